```python
import math
import jax, jax.numpy as jnp
from jax import lax
import numpy as np

D_MODEL = 2048
BATCH = 8
SEQ = 4096
DEPTH = 4

M_HEADS = 4
M_DQK = 128
M_DV = 256
M_CHUNK = 64
M_CONV = 4
M_QK_W = 2 * M_HEADS * M_DQK
M_V_W = M_HEADS * M_DV
A_HEADS = 8
A_DH = 128
A_DV = 2 * A_DH
A_Q_W = A_HEADS * 2 * A_DH
A_V_W = A_HEADS * A_DV
Q_BLOCK = 128
ROPE_THETA = 500000.0
ROPE_DIM = A_DH // 4
D_FF = -(-8 * D_MODEL // (3 * 256)) * 256
EPS = 1e-6
COL_SIZES = (M_QK_W, M_V_W, M_V_W, M_HEADS, M_HEADS, A_Q_W, A_Q_W, A_V_W, D_MODEL, D_MODEL)
D_IN = sum(COL_SIZES)

kernel_name = 'hybrid_mlstm_diffattn_block'


def rms_norm(x, g):
    xf = x.astype(jnp.float32)
    y = xf * lax.rsqrt(jnp.mean(xf * xf, axis=-1, keepdims=True) + EPS)
    return (y * g.astype(jnp.float32)).astype(x.dtype)


def split_cols(t, sizes):
    out, start = [], 0
    for s in sizes:
        out.append(t[..., start:start + s])
        start += s
    return out


def causal_dwconv(x, w, b):
    y = lax.conv_general_dilated(x, w[:, None, :].astype(x.dtype), window_strides=(1,),
                                 padding=((M_CONV - 1, 0),),
                                 dimension_numbers=('NWC', 'WIO', 'NWC'),
                                 feature_group_count=x.shape[-1])
    return y + b.astype(x.dtype)


def partial_rope(x, positions):
    half = ROPE_DIM // 2
    inv = jnp.power(ROPE_THETA, -jnp.arange(half, dtype=jnp.float32) * (2.0 / ROPE_DIM))
    ang = positions.astype(jnp.float32)[..., None] * inv
    cos = jnp.cos(ang)[:, :, None, None, :]
    sin = jnp.sin(ang)[:, :, None, None, :]
    xf = x.astype(jnp.float32)
    x1, x2, xp = xf[..., :half], xf[..., half:ROPE_DIM], xf[..., ROPE_DIM:]
    out = jnp.concatenate([x1 * cos - x2 * sin, x2 * cos + x1 * sin, xp], axis=-1)
    return out.astype(x.dtype)


def mlstm_chunkwise(q, k, v, log_i, log_f):
    B, S, H, dk = q.shape
    dv = v.shape[-1]
    L = M_CHUNK
    nc = S // L

    def to_chunks(t):
        t = t.reshape((B, nc, L, H) + t.shape[3:])
        return jnp.moveaxis(t, (1, 3), (0, 2))

    causal = jnp.tril(jnp.ones((L, L), dtype=bool))

    def step(carry, xs):
        C, n, m = carry
        qc, kc, vc, lic, lfc = xs
        b = jnp.cumsum(lfc, axis=-1)
        dmat = jnp.where(causal, b[..., :, None] - b[..., None, :] + lic[..., None, :], -jnp.inf)
        m_inter = b + m[..., None]
        m_t = jnp.maximum(jnp.max(dmat, axis=-1), m_inter)
        w_intra = jnp.exp(dmat - m_t[..., None])
        w_inter = jnp.exp(m_inter - m_t)
        s = jnp.einsum('bhtd,bhsd->bhts', qc, kc) * w_intra
        num = jnp.einsum('bhts,bhsv->bhtv', s, vc) + w_inter[..., None] * jnp.einsum('bhtd,bhdv->bhtv', qc, C)
        den = jnp.sum(s, axis=-1) + w_inter * jnp.einsum('bhtd,bhd->bht', qc, n)
        h = num / jnp.maximum(jnp.abs(den), jnp.exp(-m_t))[..., None]
        g = b[..., -1]
        a = g[..., None] - b + lic
        m_new = jnp.maximum(g + m, jnp.max(a, axis=-1))
        decay = jnp.exp(g + m - m_new)
        w_s = jnp.exp(a - m_new[..., None])
        C = decay[..., None, None] * C + jnp.einsum('bhs,bhsd,bhsv->bhdv', w_s, kc, vc)
        n = decay[..., None] * n + jnp.einsum('bhs,bhsd->bhd', w_s, kc)
        return (C, n, m_new), h

    init = (jnp.zeros((B, H, dk, dv), jnp.float32), jnp.zeros((B, H, dk), jnp.float32),
            jnp.zeros((B, H), jnp.float32))
    xs = (to_chunks(q), to_chunks(k), to_chunks(v), to_chunks(log_i), to_chunks(log_f))
    _, h = lax.scan(step, init, xs)
    return jnp.moveaxis(h, (0, 2), (1, 3)).reshape(B, S, H, dv)


def diff_attention(q, k, v, lam, lam_init, g_sub):
    B, S, H, _, dh = q.shape
    dv = v.shape[-1]
    nq = S // Q_BLOCK
    scale = dh ** -0.5
    qh = jnp.moveaxis(q, 1, 3)
    kh = jnp.moveaxis(k, 1, 3)
    vh = jnp.moveaxis(v, 1, 2)
    qb = jnp.moveaxis(qh.reshape(B, H, 2, nq, Q_BLOCK, dh), 3, 0)
    kpos = jnp.arange(S)

    def block(args):
        qblk, i = args
        s = jnp.einsum('bhcqd,bhckd->bhcqk', qblk, kh).astype(jnp.float32) * scale
        qpos = i * Q_BLOCK + jnp.arange(Q_BLOCK)
        mask = kpos[None, :] <= qpos[:, None]
        p = jax.nn.softmax(jnp.where(mask, s, -jnp.inf), axis=-1)
        attn = p[:, :, 0] - lam * p[:, :, 1]
        return jnp.einsum('bhqk,bhkv->bhqv', attn.astype(vh.dtype), vh)

    o = lax.map(block, (qb, jnp.arange(nq)))
    o = jnp.moveaxis(o, 0, 2).reshape(B, H, S, dv)
    o = rms_norm(o, g_sub) * (1.0 - lam_init)
    return jnp.moveaxis(o, 1, 2).reshape(B, S, H * dv)


def setup_inputs(seed: int = 0) -> dict:
    key = jax.random.key(seed)
    ks = jax.random.split(key, 24)
    nrm = lambda k, shape, s: jax.random.normal(k, shape, jnp.float32) * s
    x = jax.random.normal(ks[0], (BATCH, SEQ, D_MODEL), jnp.float32)
    positions = jnp.broadcast_to(jnp.arange(SEQ, dtype=jnp.int32), (BATCH, SEQ))
    return {
        'x': x,
        'positions': positions,
        'g_mix': 1.0 + nrm(ks[1], (DEPTH, D_MODEL), 0.01),
        'w_in': nrm(ks[2], (DEPTH, D_MODEL, D_IN), D_MODEL ** -0.5),
        'conv_w': nrm(ks[3], (DEPTH, M_CONV, M_QK_W), M_CONV ** -0.5),
        'conv_b': nrm(ks[4], (DEPTH, M_QK_W), 0.01),
        'i_bias': nrm(ks[5], (DEPTH, M_HEADS), 0.1),
        'f_bias': jnp.linspace(3.0, 6.0, M_HEADS, dtype=jnp.float32)[None, :] + nrm(ks[6], (DEPTH, M_HEADS), 0.1),
        'g_mhead': 1.0 + nrm(ks[7], (DEPTH, M_V_W), 0.01),
        'lambda_q1': nrm(ks[8], (DEPTH, A_DH), 0.1),
        'lambda_k1': nrm(ks[9], (DEPTH, A_DH), 0.1),
        'lambda_q2': nrm(ks[10], (DEPTH, A_DH), 0.1),
        'lambda_k2': nrm(ks[11], (DEPTH, A_DH), 0.1),
        'g_sub': 1.0 + nrm(ks[12], (DEPTH, A_DV), 0.01),
        'p_m': nrm(ks[13], (DEPTH, M_V_W, D_MODEL), M_V_W ** -0.5),
        'p_a': nrm(ks[14], (DEPTH, A_V_W, D_MODEL), A_V_W ** -0.5),
        'w_out': nrm(ks[15], (DEPTH, D_MODEL, D_MODEL), D_MODEL ** -0.5),
        'g_ffn': 1.0 + nrm(ks[16], (DEPTH, D_MODEL), 0.01),
        'w_gate': nrm(ks[17], (DEPTH, D_MODEL, D_FF), D_MODEL ** -0.5),
        'w_up': nrm(ks[18], (DEPTH, D_MODEL, D_FF), D_MODEL ** -0.5),
        'w_down': nrm(ks[19], (DEPTH, D_FF, D_MODEL), D_FF ** -0.5),
        'g_final': 1.0 + nrm(ks[20], (D_MODEL,), 0.01),
    }


def reference(x, positions, g_mix, w_in, conv_w, conv_b, i_bias, f_bias, g_mhead,
              lambda_q1, lambda_k1, lambda_q2, lambda_k2, g_sub, p_m, p_a, w_out,
              g_ffn, w_gate, w_up, w_down, g_final):
    B, S, _ = x.shape
    for l in range(DEPTH):
        lam_init = 0.8 - 0.6 * math.exp(-0.3 * l)
        h = rms_norm(x, g_mix[l])
        proj = h @ w_in[l]
        m_qk, m_v, m_o, m_i, m_f, a_q, a_k, a_v, gate_m, gate_a = split_cols(proj, COL_SIZES)

        m_qk = jax.nn.silu(causal_dwconv(m_qk, conv_w[l], conv_b[l]))
        m_q = m_qk[..., :M_QK_W // 2].reshape(B, S, M_HEADS, M_DQK).astype(jnp.float32)
        m_k = m_qk[..., M_QK_W // 2:].reshape(B, S, M_HEADS, M_DQK).astype(jnp.float32) * (M_DQK ** -0.5)
        m_vv = m_v.reshape(B, S, M_HEADS, M_DV).astype(jnp.float32)
        log_i = (m_i + i_bias[l]).astype(jnp.float32)
        log_f = jax.nn.log_sigmoid((m_f + f_bias[l]).astype(jnp.float32))
        h_m = mlstm_chunkwise(m_q, m_k, m_vv, log_i, log_f)
        h_m = rms_norm(h_m, g_mhead[l].reshape(M_HEADS, M_DV)).reshape(B, S, M_V_W).astype(x.dtype)
        h_m = jax.nn.sigmoid(m_o) * h_m

        q = partial_rope(a_q.reshape(B, S, A_HEADS, 2, A_DH), positions)
        k = partial_rope(a_k.reshape(B, S, A_HEADS, 2, A_DH), positions)
        lam = (jnp.exp(jnp.sum(lambda_q1[l].astype(jnp.float32) * lambda_k1[l].astype(jnp.float32)))
               - jnp.exp(jnp.sum(lambda_q2[l].astype(jnp.float32) * lambda_k2[l].astype(jnp.float32)))
               + lam_init)
        h_a = diff_attention(q, k, a_v.reshape(B, S, A_HEADS, A_DV), lam, lam_init, g_sub[l])

        y = jax.nn.sigmoid(gate_m) * (h_m @ p_m[l]) + jax.nn.sigmoid(gate_a) * (h_a @ p_a[l])
        x = x + y @ w_out[l]

        h = rms_norm(x, g_ffn[l])
        x = x + (jax.nn.silu(h @ w_gate[l]) * (h @ w_up[l])) @ w_down[l]
    return rms_norm(x, g_final)
```

```python
import functools
import math

import jax
import jax.numpy as jnp
from jax import lax
from jax.experimental import pallas as pl
from jax.experimental.pallas import tpu as pltpu

DEPTH = 4
M_HEADS = 4
M_DQK = 128
M_DV = 256
M_CONV = 4
M_QK_W = 2 * M_HEADS * M_DQK
M_V_W = M_HEADS * M_DV
A_HEADS = 8
A_DH = 128
A_DV = 2 * A_DH
A_Q_W = A_HEADS * 2 * A_DH
A_V_W = A_HEADS * A_DV
ROPE_THETA = 500000.0
ROPE_DIM = A_DH // 4
EPS = 1e-6

LANES = 128
VMEM_LIMIT_BYTES = 60 * 1024 * 1024

C_MQK = 0
C_MV = C_MQK + M_QK_W
C_MO = C_MV + M_V_W
C_AQ = C_MO + M_V_W
C_AK = C_AQ + A_Q_W
C_AV = C_AK + A_Q_W
C_GM = C_AV + A_V_W
PROJ_W = C_GM + 2 * 2048

F32 = jnp.float32
BF16 = jnp.bfloat16


def _cparams(*sem):
    return pltpu.CompilerParams(dimension_semantics=sem, vmem_limit_bytes=VMEM_LIMIT_BYTES)


def _tile(n, pref):
    t = min(n, pref)
    while n % t:
        t //= 2
    return t


def _rmsnorm_body(x_ref, g_ref, o_ref):
    x = x_ref[...].astype(F32)
    r = lax.rsqrt(jnp.mean(x * x, axis=-1, keepdims=True) + EPS)
    o_ref[...] = ((x * r) * g_ref[...]).astype(o_ref.dtype)


def _rmsnorm(x, g, out_dtype):
    m, d = x.shape
    tm = _tile(m, 1024)
    return pl.pallas_call(
        _rmsnorm_body,
        grid=(m // tm,),
        in_specs=[pl.BlockSpec((tm, d), lambda i: (i, 0)), pl.BlockSpec((1, d), lambda i: (0, 0))],
        out_specs=pl.BlockSpec((tm, d), lambda i: (i, 0)),
        out_shape=jax.ShapeDtypeStruct((m, d), out_dtype),
        compiler_params=_cparams("parallel"),
        name="rmsnorm",
    )(x, g.reshape(1, d).astype(F32))


def _mm_body(a_ref, w_ref, o_ref):
    o_ref[...] = jnp.dot(a_ref[...], w_ref[...], preferred_element_type=F32).astype(o_ref.dtype)


def _matmul(a, w, out_dtype, tm_pref=1024, tn_pref=1024, name="matmul"):
    m, k = a.shape
    n = w.shape[1]
    tm, tn = _tile(m, tm_pref), _tile(n, tn_pref)
    return pl.pallas_call(
        _mm_body,
        grid=(m // tm, n // tn),
        in_specs=[pl.BlockSpec((tm, k), lambda i, j: (i, 0)), pl.BlockSpec((k, tn), lambda i, j: (0, j))],
        out_specs=pl.BlockSpec((tm, tn), lambda i, j: (i, j)),
        out_shape=jax.ShapeDtypeStruct((m, n), out_dtype),
        compiler_params=_cparams("parallel", "arbitrary"),
        name=name,
    )(a, w)


def _mm_res_body(a_ref, w_ref, x_ref, o_ref):
    o_ref[...] = x_ref[...] + jnp.dot(a_ref[...], w_ref[...], preferred_element_type=F32)


def _matmul_residual(a, w, x, tm_pref, tn_pref, name):
    m, k = a.shape
    n = w.shape[1]
    tm, tn = _tile(m, tm_pref), _tile(n, tn_pref)
    return pl.pallas_call(
        _mm_res_body,
        grid=(m // tm, n // tn),
        in_specs=[pl.BlockSpec((tm, k), lambda i, j: (i, 0)),
                  pl.BlockSpec((k, tn), lambda i, j: (0, j)),
                  pl.BlockSpec((tm, tn), lambda i, j: (i, j))],
        out_specs=pl.BlockSpec((tm, tn), lambda i, j: (i, j)),
        out_shape=jax.ShapeDtypeStruct((m, n), F32),
        input_output_aliases={2: 0},
        compiler_params=_cparams("parallel", "arbitrary"),
        name=name,
    )(a, w, x)


def _swiglu_body(h_ref, wg_ref, wu_ref, o_ref):
    h = h_ref[...]
    g = jnp.dot(h, wg_ref[...], preferred_element_type=F32)
    u = jnp.dot(h, wu_ref[...], preferred_element_type=F32)
    o_ref[...] = ((g * jax.nn.sigmoid(g)) * u).astype(o_ref.dtype)


def _swiglu_up(h, wg, wu):
    m, k = h.shape
    n = wg.shape[1]
    tm, tn = _tile(m, 1024), _tile(n, 512)
    return pl.pallas_call(
        _swiglu_body,
        grid=(m // tm, n // tn),
        in_specs=[pl.BlockSpec((tm, k), lambda i, j: (i, 0)),
                  pl.BlockSpec((k, tn), lambda i, j: (0, j)),
                  pl.BlockSpec((k, tn), lambda i, j: (0, j))],
        out_specs=pl.BlockSpec((tm, tn), lambda i, j: (i, j)),
        out_shape=jax.ShapeDtypeStruct((m, n), BF16),
        compiler_params=_cparams("parallel", "arbitrary"),
        name="swiglu_up",
    )(h, wg, wu)


def _merge_body(hm_ref, ha_ref, pm_ref, pa_ref, gm_ref, ga_ref, o_ref):
    ym = jnp.dot(hm_ref[...], pm_ref[...], preferred_element_type=F32)
    ya = jnp.dot(ha_ref[...], pa_ref[...], preferred_element_type=F32)
    gm = jax.nn.sigmoid(gm_ref[...].astype(F32))
    ga = jax.nn.sigmoid(ga_ref[...].astype(F32))
    o_ref[...] = (gm * ym + ga * ya).astype(o_ref.dtype)


def _gated_merge(h_m, h_a, p_m, p_a, proj):
    m = h_m.shape[0]
    n = p_m.shape[1]
    tm, tn = _tile(m, 1024), _tile(n, 512)
    gm0, ga0 = C_GM // tn, (C_GM + n) // tn
    return pl.pallas_call(
        _merge_body,
        grid=(m // tm, n // tn),
        in_specs=[pl.BlockSpec((tm, h_m.shape[1]), lambda i, j: (i, 0)),
                  pl.BlockSpec((tm, h_a.shape[1]), lambda i, j: (i, 0)),
                  pl.BlockSpec((p_m.shape[0], tn), lambda i, j: (0, j)),
                  pl.BlockSpec((p_a.shape[0], tn), lambda i, j: (0, j)),
                  pl.BlockSpec((tm, tn), lambda i, j: (i, gm0 + j)),
                  pl.BlockSpec((tm, tn), lambda i, j: (i, ga0 + j))],
        out_specs=pl.BlockSpec((tm, tn), lambda i, j: (i, j)),
        out_shape=jax.ShapeDtypeStruct((m, n), BF16),
        compiler_params=_cparams("parallel", "arbitrary"),
        name="gated_merge",
    )(h_m, h_a, p_m, p_a, proj, proj)


def _rope_table_body(pos_ref, inv_ref, cos_ref, sin_lo_ref, sin_hi_ref):
    ang = pos_ref[...].astype(F32) * inv_ref[...]
    lane = lax.broadcasted_iota(jnp.int32, ang.shape, 1)
    half = ROPE_DIM // 2
    s = jnp.sin(ang)
    cos_ref[...] = jnp.cos(ang)
    sin_lo_ref[...] = jnp.where(lane < half, -s, 0.0)
    sin_hi_ref[...] = jnp.where((lane >= half) & (lane < ROPE_DIM), s, 0.0)


def _rope_tables(positions):
    m = positions.size
    half = ROPE_DIM // 2
    inv = jnp.power(ROPE_THETA, -jnp.arange(half, dtype=F32) * (2.0 / ROPE_DIM))
    inv_lane = jnp.zeros((1, LANES), F32).at[0, :ROPE_DIM].set(jnp.concatenate([inv, inv]))
    ts = _tile(m, 2048)
    spec = pl.BlockSpec((ts, LANES), lambda i: (i, 0))
    shape = jax.ShapeDtypeStruct((m, LANES), F32)
    return pl.pallas_call(
        _rope_table_body,
        grid=(m // ts,),
        in_specs=[pl.BlockSpec((ts, 1), lambda i: (i, 0)), pl.BlockSpec((1, LANES), lambda i: (0, 0))],
        out_specs=[spec, spec, spec],
        out_shape=[shape, shape, shape],
        compiler_params=_cparams("parallel"),
        name="rope_tables",
    )(positions.reshape(m, 1), inv_lane)


def _rope_body(x_ref, cos_ref, slo_ref, shi_ref, o_ref, *, q_blocks, scale):
    mul = jnp.where(pl.program_id(1) < q_blocks, scale, 1.0).astype(F32)
    c = cos_ref[...] * mul
    slo = slo_ref[...] * mul
    shi = shi_ref[...] * mul
    half = ROPE_DIM // 2
    for hh in range(x_ref.shape[1] // LANES):
        sl = slice(hh * LANES, (hh + 1) * LANES)
        x = x_ref[:, sl].astype(F32)
        y = x * c + pltpu.roll(x, LANES - half, 1) * slo + pltpu.roll(x, half, 1) * shi
        o_ref[:, sl] = y.astype(o_ref.dtype)


def _rope(proj, tables):
    m = proj.shape[0]
    tw = 1024
    ts = _tile(m, 1024)
    nblk = 2 * A_Q_W // tw
    tspec = pl.BlockSpec((ts, LANES), lambda i, j: (i, 0))
    body = functools.partial(_rope_body, q_blocks=A_Q_W // tw, scale=A_DH ** -0.5)
    return pl.pallas_call(
        body,
        grid=(m // ts, nblk),
        in_specs=[pl.BlockSpec((ts, tw), lambda i, j: (i, C_AQ // tw + j)), tspec, tspec, tspec],
        out_specs=pl.BlockSpec((ts, tw), lambda i, j: (i, j)),
        out_shape=jax.ShapeDtypeStruct((m, 2 * A_Q_W), BF16),
        compiler_params=_cparams("parallel", "arbitrary"),
        name="rope",
    )(proj, *tables)


def _conv_silu_body(x_ref, w_ref, b_ref, o_ref, *, k_blocks_from, k_scale):
    x = x_ref[0].astype(F32)
    w = w_ref[...]
    row = lax.broadcasted_iota(jnp.int32, x.shape, 0)
    y = x * w[M_CONV - 1:M_CONV, :] + b_ref[...]
    for d in range(1, M_CONV):
        xd = jnp.where(row >= d, pltpu.roll(x, d, 0), 0.0)
        y = y + xd * w[M_CONV - 1 - d:M_CONV - d, :]
    y = y * jax.nn.sigmoid(y)
    mul = jnp.where(pl.program_id(1) >= k_blocks_from, k_scale, 1.0).astype(F32)
    o_ref[0] = (y * mul).astype(o_ref.dtype)


def _conv_silu(proj3, conv_w, conv_b):
    b, s, _ = proj3.shape
    nblk = M_QK_W // LANES
    body = functools.partial(_conv_silu_body, k_blocks_from=nblk // 2, k_scale=M_DQK ** -0.5)
    return pl.pallas_call(
        body,
        grid=(b, nblk),
        in_specs=[pl.BlockSpec((1, s, LANES), lambda i, j: (i, 0, C_MQK // LANES + j)),
                  pl.BlockSpec((M_CONV, LANES), lambda i, j: (0, j)),
                  pl.BlockSpec((1, LANES), lambda i, j: (0, j))],
        out_specs=pl.BlockSpec((1, s, LANES), lambda i, j: (i, 0, j)),
        out_shape=jax.ShapeDtypeStruct((b, s, M_QK_W), BF16),
        compiler_params=_cparams("parallel", "arbitrary"),
        name="conv_silu",
    )(proj3, conv_w.astype(F32), conv_b.reshape(1, M_QK_W).astype(F32))


def _log_sigmoid(x):
    return jnp.minimum(x, 0.0) - jnp.log1p(jnp.exp(-jnp.abs(x)))


def _mlstm_body(qk_ref, v_ref, o_ref, gc_ref, gr_ref, bc_ref, br_ref, gn_ref, out_ref,
                c_ref, n_ref, m_ref, *, chunk):
    L = chunk

    @pl.when(pl.program_id(1) == 0)
    def _():
        c_ref[...] = jnp.zeros_like(c_ref)
        n_ref[...] = jnp.zeros_like(n_ref)
        m_ref[...] = jnp.zeros_like(m_ref)

    gc = gc_ref[0] + bc_ref[...]
    gr = gr_ref[0] + br_ref[...]
    row = lax.broadcasted_iota(jnp.int32, (L, L), 0)
    col = lax.broadcasted_iota(jnp.int32, (L, L), 1)
    causal = col <= row
    for h in range(M_HEADS):
        li_c = gc[:, h:h + 1]
        lf_c = _log_sigmoid(gc[:, M_HEADS + h:M_HEADS + h + 1])
        li_r = gr[h:h + 1, :]
        lf_r = _log_sigmoid(gr[M_HEADS + h:M_HEADS + h + 1, :])
        b_c = jnp.sum(jnp.where(causal, lf_r, 0.0), axis=1, keepdims=True)
        b_r = jnp.sum(jnp.where(row <= col, lf_c, 0.0), axis=0, keepdims=True)
        g = jnp.sum(lf_r, axis=1, keepdims=True)
        m_prev = m_ref[h][0:1, 0:1]
        dmat = jnp.where(causal, b_c - b_r + li_r, -jnp.inf)
        m_inter = b_c + m_prev
        m_t = jnp.maximum(jnp.max(dmat, axis=1, keepdims=True), m_inter)
        w_intra = jnp.exp(dmat - m_t)
        w_inter = jnp.exp(m_inter - m_t)
        q = qk_ref[0, :, h * M_DQK:(h + 1) * M_DQK]
        k = qk_ref[0, :, M_QK_W // 2 + h * M_DQK:M_QK_W // 2 + (h + 1) * M_DQK]
        v = v_ref[0, :, h * M_DV:(h + 1) * M_DV]
        s = lax.dot_general(q, k, (((1,), (1,)), ((), ())), preferred_element_type=F32) * w_intra
        c_old = c_ref[h]
        n_old = n_ref[h][0:1, :]
        num = (jnp.dot(s.astype(BF16), v, preferred_element_type=F32)
               + w_inter * jnp.dot(q, c_old.astype(BF16), preferred_element_type=F32))
        den = (jnp.sum(s, axis=1, keepdims=True)
               + w_inter * jnp.sum(q.astype(F32) * n_old, axis=1, keepdims=True))
        ht = num / jnp.maximum(jnp.abs(den), jnp.exp(-m_t))
        a_c = g - b_c + li_c
        m_new = jnp.maximum(g + m_prev, jnp.max(a_c, axis=0, keepdims=True))
        decay = jnp.exp(g + m_prev - m_new)
        kw = k.astype(F32) * jnp.exp(a_c - m_new)
        c_ref[h] = decay * c_old + lax.dot_general(
            kw.astype(BF16), v, (((0,), (0,)), ((), ())), preferred_element_type=F32)
        n_ref[h] = jnp.broadcast_to(decay * n_old + jnp.sum(kw, axis=0, keepdims=True), n_ref.shape[1:])
        m_ref[h] = jnp.broadcast_to(m_new, m_ref.shape[1:])
        hn = ht * lax.rsqrt(jnp.mean(ht * ht, axis=-1, keepdims=True) + EPS)
        hn = hn * gn_ref[:, h * M_DV:(h + 1) * M_DV]
        og = jax.nn.sigmoid(o_ref[0, :, h * M_DV:(h + 1) * M_DV].astype(F32))
        out_ref[0, :, h * M_DV:(h + 1) * M_DV] = (og * hn).astype(out_ref.dtype)


def _mlstm(qk, proj3, gates_c, gates_r, bias_c, bias_r, g_mhead, chunk):
    b, s, _ = proj3.shape
    L = chunk
    body = functools.partial(_mlstm_body, chunk=L)
    return pl.pallas_call(
        body,
        grid=(b, s // L),
        in_specs=[pl.BlockSpec((1, L, M_QK_W), lambda i, c: (i, c, 0)),
                  pl.BlockSpec((1, L, M_V_W), lambda i, c: (i, c, C_MV // M_V_W)),
                  pl.BlockSpec((1, L, M_V_W), lambda i, c: (i, c, C_MO // M_V_W)),
                  pl.BlockSpec((1, L, LANES), lambda i, c: (i, c, 0)),
                  pl.BlockSpec((1, 2 * M_HEADS, L), lambda i, c: (i, 0, c)),
                  pl.BlockSpec((1, LANES), lambda i, c: (0, 0)),
                  pl.BlockSpec((2 * M_HEADS, 1), lambda i, c: (0, 0)),
                  pl.BlockSpec((1, M_V_W), lambda i, c: (0, 0))],
        out_specs=pl.BlockSpec((1, L, M_V_W), lambda i, c: (i, c, 0)),
        out_shape=jax.ShapeDtypeStruct((b, s, M_V_W), BF16),
        scratch_shapes=[pltpu.VMEM((M_HEADS, M_DQK, M_DV), F32),
                        pltpu.VMEM((M_HEADS, 8, M_DQK), F32),
                        pltpu.VMEM((M_HEADS, 8, LANES), F32)],
        compiler_params=_cparams("parallel", "arbitrary"),
        name="mlstm",
    )(qk, proj3, proj3, gates_c, gates_r, bias_c, bias_r, g_mhead.reshape(1, M_V_W).astype(F32))


def _attn_step(q_ref, k_ref, v_ref, m_ref, l_ref, acc_ref, masked):
    q = q_ref[0]
    k = k_ref[0]
    v = v_ref[0]
    tq, tk = q.shape[0], k.shape[0]
    if masked:
        keep = (lax.broadcasted_iota(jnp.int32, (tq, tk), 1)
                <= lax.broadcasted_iota(jnp.int32, (tq, tk), 0))
    for c in range(2):
        sl = slice(c * A_DH, (c + 1) * A_DH)
        s = lax.dot_general(q[:, sl], k[:, sl], (((1,), (1,)), ((), ())), preferred_element_type=F32)
        if masked:
            s = jnp.where(keep, s, -jnp.inf)
        m_old = m_ref[c]
        m_new = jnp.maximum(m_old, jnp.max(s, axis=1, keepdims=True))
        alpha = jnp.exp(m_old - m_new)
        p = jnp.exp(s - m_new)
        l_ref[c] = alpha * l_ref[c] + jnp.sum(p, axis=1, keepdims=True)
        acc_ref[c] = alpha * acc_ref[c] + jnp.dot(p.astype(BF16), v, preferred_element_type=F32)
        m_ref[c] = m_new


def _attn_body(qi_tab, ki_tab, q_ref, k_ref, v_ref, lam_ref, g_ref, o_ref, m_ref, l_ref, acc_ref,
               *, lam_init):
    p = pl.program_id(2)
    qi = qi_tab[p]
    ki = ki_tab[p]

    @pl.when(ki == 0)
    def _():
        m_ref[...] = jnp.full_like(m_ref, -jnp.inf)
        l_ref[...] = jnp.zeros_like(l_ref)
        acc_ref[...] = jnp.zeros_like(acc_ref)

    @pl.when(ki < qi)
    def _():
        _attn_step(q_ref, k_ref, v_ref, m_ref, l_ref, acc_ref, masked=False)

    @pl.when(ki == qi)
    def _():
        _attn_step(q_ref, k_ref, v_ref, m_ref, l_ref, acc_ref, masked=True)
        lv = lam_ref[...]
        lam = (jnp.exp(jnp.sum(lv[0:1] * lv[1:2], axis=1, keepdims=True))
               - jnp.exp(jnp.sum(lv[2:3] * lv[3:4], axis=1, keepdims=True)) + lam_init)
        o = acc_ref[0] / l_ref[0] - lam * (acc_ref[1] / l_ref[1])
        y = (o * lax.rsqrt(jnp.mean(o * o, axis=-1, keepdims=True) + EPS)) * g_ref[...]
        o_ref[0] = (y * (1.0 - lam_init)).astype(o_ref.dtype)


def _diff_attention(qk_rot, proj3, lam_vecs, g_sub, lam_init, tq):
    b, s, _ = qk_rot.shape
    nq = s // tq
    pairs = [(qi, ki) for qi in range(nq) for ki in range(qi + 1)]
    qi_tab = jnp.array([pq for pq, _ in pairs], jnp.int32)
    ki_tab = jnp.array([pk for _, pk in pairs], jnp.int32)
    w = 2 * A_DH
    k0 = A_Q_W // w
    v0 = C_AV // w
    grid_spec = pltpu.PrefetchScalarGridSpec(
        num_scalar_prefetch=2,
        grid=(b, A_HEADS, len(pairs)),
        in_specs=[pl.BlockSpec((1, tq, w), lambda i, h, p, qt, kt: (i, qt[p], h)),
                  pl.BlockSpec((1, tq, w), lambda i, h, p, qt, kt: (i, kt[p], k0 + h)),
                  pl.BlockSpec((1, tq, A_DV), lambda i, h, p, qt, kt: (i, kt[p], v0 + h)),
                  pl.BlockSpec((4, A_DH), lambda i, h, p, qt, kt: (0, 0)),
                  pl.BlockSpec((1, A_DV), lambda i, h, p, qt, kt: (0, 0))],
        out_specs=pl.BlockSpec((1, tq, A_DV), lambda i, h, p, qt, kt: (i, qt[p], h)),
        scratch_shapes=[pltpu.VMEM((2, tq, 1), F32), pltpu.VMEM((2, tq, 1), F32),
                        pltpu.VMEM((2, tq, A_DV), F32)],
    )
    return pl.pallas_call(
        functools.partial(_attn_body, lam_init=lam_init),
        grid_spec=grid_spec,
        out_shape=jax.ShapeDtypeStruct((b, s, A_V_W), BF16),
        compiler_params=_cparams("parallel", "parallel", "arbitrary"),
        name="diff_attention",
    )(qi_tab, ki_tab, qk_rot, qk_rot, proj3, lam_vecs, g_sub.reshape(1, A_DV).astype(F32))


def kernel(x, positions, g_mix, w_in, conv_w, conv_b, i_bias, f_bias, g_mhead, lambda_q1, lambda_k1,
           lambda_q2, lambda_k2, g_sub, p_m, p_a, w_out, g_ffn, w_gate, w_up, w_down, g_final):
    B, S, D = x.shape
    M = B * S
    xf = x.reshape(M, D)
    tables = _rope_tables(positions)
    gate0 = C_AQ
    chunk = _tile(S, 128)
    tq = _tile(S, 512)
    for l in range(DEPTH):
        lam_init = 0.8 - 0.6 * math.exp(-0.3 * l)
        w = w_in[l]
        w_main = jnp.concatenate([w[:, :gate0], w[:, gate0 + 2 * M_HEADS:]], axis=1).astype(BF16)
        w_gates = jnp.pad(w[:, gate0:gate0 + 2 * M_HEADS], ((0, 0), (0, LANES - 2 * M_HEADS))).astype(BF16)

        h = _rmsnorm(xf, g_mix[l], BF16)
        proj = _matmul(h, w_main, BF16, name="in_proj")
        gates = _matmul(h, w_gates, F32, tn_pref=LANES, name="gate_proj")
        proj3 = proj.reshape(B, S, PROJ_W)

        qk = _conv_silu(proj3, conv_w[l], conv_b[l])
        gates_c = gates.reshape(B, S, LANES)
        gates_r = jnp.swapaxes(gates_c[:, :, :2 * M_HEADS], 1, 2)
        bias8 = jnp.concatenate([i_bias[l], f_bias[l]]).astype(F32)
        bias_c = jnp.pad(bias8, (0, LANES - 2 * M_HEADS)).reshape(1, LANES)
        h_m = _mlstm(qk, proj3, gates_c, gates_r, bias_c, bias8.reshape(2 * M_HEADS, 1), g_mhead[l], chunk)

        qk_rot = _rope(proj, tables).reshape(B, S, 2 * A_Q_W)
        lam_vecs = jnp.stack([lambda_q1[l], lambda_k1[l], lambda_q2[l], lambda_k2[l]]).astype(F32)
        h_a = _diff_attention(qk_rot, proj3, lam_vecs, g_sub[l], lam_init, tq)

        y = _gated_merge(h_m.reshape(M, M_V_W), h_a.reshape(M, A_V_W),
                         p_m[l].astype(BF16), p_a[l].astype(BF16), proj)
        xf = _matmul_residual(y, w_out[l].astype(BF16), xf, 1024, 1024, "out_proj")

        h = _rmsnorm(xf, g_ffn[l], BF16)
        hh = _swiglu_up(h, w_gate[l].astype(BF16), w_up[l].astype(BF16))
        xf = _matmul_residual(hh, w_down[l].astype(BF16), xf, 1024, 512, "ffn_down")
    return _rmsnorm(xf, g_final, x.dtype).reshape(B, S, D)
```

```python
import functools
import math

import jax
import jax.numpy as jnp
from jax import lax
from jax.experimental import pallas as pl
from jax.experimental.pallas import tpu as pltpu

DEPTH = 4
M_HEADS = 4
M_DQK = 128
M_DV = 256
M_CONV = 4
M_QK_W = 2 * M_HEADS * M_DQK
M_V_W = M_HEADS * M_DV
A_HEADS = 8
A_DH = 128
A_DV = 2 * A_DH
A_Q_W = A_HEADS * 2 * A_DH
A_V_W = A_HEADS * A_DV
ROPE_THETA = 500000.0
ROPE_DIM = A_DH // 4
EPS = 1e-6

LANES = 128
VMEM_LIMIT_BYTES = 60 * 1024 * 1024

C_MQK = 0
C_MV = C_MQK + M_QK_W
C_MO = C_MV + M_V_W
C_AQ = C_MO + M_V_W
C_AK = C_AQ + A_Q_W
C_AV = C_AK + A_Q_W
C_GM = C_AV + A_V_W
PROJ_W = C_GM + 2 * 2048

F32 = jnp.float32
BF16 = jnp.bfloat16


def _cparams(*sem, flags=None):
    return pltpu.CompilerParams(dimension_semantics=sem, vmem_limit_bytes=VMEM_LIMIT_BYTES, flags=flags)


def _tile(n, pref):
    t = min(n, pref)
    while n % t:
        t //= 2
    return t


def _rmsnorm_body(x_ref, g_ref, o_ref):
    x = x_ref[...].astype(F32)
    r = lax.rsqrt(jnp.mean(x * x, axis=-1, keepdims=True) + EPS)
    o_ref[...] = ((x * r) * g_ref[...]).astype(o_ref.dtype)


def _rmsnorm(x, g, out_dtype):
    m, d = x.shape
    tm = _tile(m, 1024)
    return pl.pallas_call(
        _rmsnorm_body,
        grid=(m // tm,),
        in_specs=[pl.BlockSpec((tm, d), lambda i: (i, 0)), pl.BlockSpec((1, d), lambda i: (0, 0))],
        out_specs=pl.BlockSpec((tm, d), lambda i: (i, 0)),
        out_shape=jax.ShapeDtypeStruct((m, d), out_dtype),
        compiler_params=_cparams("parallel"),
        name="rmsnorm",
    )(x, g.reshape(1, d).astype(F32))


def _normed_bf16(x_ref, g_ref):
    x = x_ref[...]
    r = lax.rsqrt(jnp.mean(x * x, axis=-1, keepdims=True) + EPS)
    return ((x * r) * g_ref[...]).astype(BF16)


def _in_proj_body(x_ref, g_ref, w_ref, wg_ref, cos_ref, sin_ref, o_ref, gates_ref, h_ref,
                  *, rope_lo, rope_q_hi, rope_hi, q_scale):
    j = pl.program_id(1)

    @pl.when(j == 0)
    def _():
        h = _normed_bf16(x_ref, g_ref)
        h_ref[...] = h
        gates_ref[...] = jnp.dot(h, wg_ref[...], preferred_element_type=F32)

    is_rope = (j >= rope_lo) & (j < rope_hi)

    @pl.when(jnp.logical_not(is_rope))
    def _():
        o_ref[...] = jnp.dot(h_ref[...], w_ref[...], preferred_element_type=F32).astype(o_ref.dtype)

    @pl.when(is_rope)
    def _():
        acc = jnp.dot(h_ref[...], w_ref[...], preferred_element_type=F32)
        mul = jnp.where(j < rope_q_hi, q_scale, 1.0).astype(F32)
        c = cos_ref[...] * mul
        s = sin_ref[...] * mul
        for hh in range(acc.shape[1] // LANES):
            sl = slice(hh * LANES, (hh + 1) * LANES)
            a = acc[:, sl]
            o_ref[:, sl] = (a * c + pltpu.roll(a, LANES // 2, 1) * s).astype(o_ref.dtype)


def _in_proj(x, g, w_main, w_gates, tables):
    m, d = x.shape
    n = w_main.shape[1]
    tm, tn = _tile(m, 1024), 1024
    assert C_AQ % tn == 0 and C_AK % tn == 0 and C_AV % tn == 0 and n % tn == 0
    tspec = pl.BlockSpec((tm, LANES), lambda i, j: (i, 0))
    body = functools.partial(_in_proj_body, rope_lo=C_AQ // tn, rope_q_hi=C_AK // tn, rope_hi=C_AV // tn,
                             q_scale=A_DH ** -0.5 * math.log2(math.e))
    return pl.pallas_call(
        body,
        grid=(m // tm, n // tn),
        in_specs=[pl.BlockSpec((tm, d), lambda i, j: (i, 0)),
                  pl.BlockSpec((1, d), lambda i, j: (0, 0)),
                  pl.BlockSpec((d, tn), lambda i, j: (0, j)),
                  pl.BlockSpec((d, LANES), lambda i, j: (0, 0)),
                  tspec, tspec],
        out_specs=[pl.BlockSpec((tm, tn), lambda i, j: (i, j)),
                   pl.BlockSpec((tm, LANES), lambda i, j: (i, 0))],
        out_shape=[jax.ShapeDtypeStruct((m, n), BF16), jax.ShapeDtypeStruct((m, LANES), F32)],
        scratch_shapes=[pltpu.VMEM((tm, d), BF16)],
        compiler_params=_cparams("parallel", "arbitrary"),
        name="in_proj",
    )(x, g.reshape(1, d).astype(F32), w_main, w_gates, *tables)


def _mm_res_body(a_ref, w_ref, x_ref, o_ref):
    o_ref[...] = x_ref[...] + jnp.dot(a_ref[...], w_ref[...], preferred_element_type=F32)


def _matmul_residual(a, w, x, tm_pref, tn_pref, name):
    m, k = a.shape
    n = w.shape[1]
    tm, tn = _tile(m, tm_pref), _tile(n, tn_pref)
    return pl.pallas_call(
        _mm_res_body,
        grid=(m // tm, n // tn),
        in_specs=[pl.BlockSpec((tm, k), lambda i, j: (i, 0)),
                  pl.BlockSpec((k, tn), lambda i, j: (0, j)),
                  pl.BlockSpec((tm, tn), lambda i, j: (i, j))],
        out_specs=pl.BlockSpec((tm, tn), lambda i, j: (i, j)),
        out_shape=jax.ShapeDtypeStruct((m, n), F32),
        input_output_aliases={2: 0},
        compiler_params=_cparams("parallel", "arbitrary"),
        name=name,
    )(a, w, x)


def _swiglu_body(x_ref, g_ref, wg_ref, wu_ref, o_ref, h_ref):
    @pl.when(pl.program_id(1) == 0)
    def _():
        h_ref[...] = _normed_bf16(x_ref, g_ref)

    h = h_ref[...]
    g = jnp.dot(h, wg_ref[...], preferred_element_type=F32)
    u = jnp.dot(h, wu_ref[...], preferred_element_type=F32)
    o_ref[...] = ((g * jax.nn.sigmoid(g)) * u).astype(o_ref.dtype)


def _swiglu_up(x, g, wg, wu):
    m, k = x.shape
    n = wg.shape[1]
    tm, tn = _tile(m, 1024), _tile(n, 512)
    return pl.pallas_call(
        _swiglu_body,
        grid=(m // tm, n // tn),
        in_specs=[pl.BlockSpec((tm, k), lambda i, j: (i, 0)),
                  pl.BlockSpec((1, k), lambda i, j: (0, 0)),
                  pl.BlockSpec((k, tn), lambda i, j: (0, j)),
                  pl.BlockSpec((k, tn), lambda i, j: (0, j))],
        out_specs=pl.BlockSpec((tm, tn), lambda i, j: (i, j)),
        out_shape=jax.ShapeDtypeStruct((m, n), BF16),
        scratch_shapes=[pltpu.VMEM((tm, k), BF16)],
        compiler_params=_cparams("parallel", "arbitrary"),
        name="swiglu_up",
    )(x, g.reshape(1, k).astype(F32), wg, wu)


def _merge_body(hm_ref, ha_ref, pm_ref, pa_ref, gm_ref, ga_ref, o_ref):
    ym = jnp.dot(hm_ref[...], pm_ref[...], preferred_element_type=F32)
    ya = jnp.dot(ha_ref[...], pa_ref[...], preferred_element_type=F32)
    gm = jax.nn.sigmoid(gm_ref[...].astype(F32))
    ga = jax.nn.sigmoid(ga_ref[...].astype(F32))
    o_ref[...] = (gm * ym + ga * ya).astype(o_ref.dtype)


def _gated_merge(h_m, h_a, p_m, p_a, proj):
    m = h_m.shape[0]
    n = p_m.shape[1]
    tm, tn = _tile(m, 1024), _tile(n, 512)
    gm0, ga0 = C_GM // tn, (C_GM + n) // tn
    return pl.pallas_call(
        _merge_body,
        grid=(m // tm, n // tn),
        in_specs=[pl.BlockSpec((tm, h_m.shape[1]), lambda i, j: (i, 0)),
                  pl.BlockSpec((tm, h_a.shape[1]), lambda i, j: (i, 0)),
                  pl.BlockSpec((p_m.shape[0], tn), lambda i, j: (0, j)),
                  pl.BlockSpec((p_a.shape[0], tn), lambda i, j: (0, j)),
                  pl.BlockSpec((tm, tn), lambda i, j: (i, gm0 + j)),
                  pl.BlockSpec((tm, tn), lambda i, j: (i, ga0 + j))],
        out_specs=pl.BlockSpec((tm, tn), lambda i, j: (i, j)),
        out_shape=jax.ShapeDtypeStruct((m, n), BF16),
        compiler_params=_cparams("parallel", "arbitrary"),
        name="gated_merge",
    )(h_m, h_a, p_m, p_a, proj, proj)


_ROPE_HALF = ROPE_DIM // 2
ROPE_LANE_TO_DIM = (list(range(0, _ROPE_HALF)) + list(range(ROPE_DIM, LANES // 2 + _ROPE_HALF))
                    + list(range(_ROPE_HALF, ROPE_DIM)) + list(range(LANES // 2 + _ROPE_HALF, A_DH)))


def _rope_table_body(pos_ref, inv_ref, cos_ref, sin_ref):
    ang = pos_ref[...].astype(F32) * inv_ref[...]
    lane = lax.broadcasted_iota(jnp.int32, ang.shape, 1)
    s = jnp.sin(ang)
    cos_ref[...] = jnp.cos(ang)
    sin_ref[...] = jnp.where(lane < LANES // 2, -s, s)


def _rope_tables(positions):
    m = positions.size
    inv = jnp.power(ROPE_THETA, -jnp.arange(_ROPE_HALF, dtype=F32) * (2.0 / ROPE_DIM))
    inv_lane = (jnp.zeros((1, LANES), F32).at[0, :_ROPE_HALF].set(inv)
                .at[0, LANES // 2:LANES // 2 + _ROPE_HALF].set(inv))
    ts = _tile(m, 2048)
    spec = pl.BlockSpec((ts, LANES), lambda i: (i, 0))
    shape = jax.ShapeDtypeStruct((m, LANES), F32)
    return pl.pallas_call(
        _rope_table_body,
        grid=(m // ts,),
        in_specs=[pl.BlockSpec((ts, 1), lambda i: (i, 0)), pl.BlockSpec((1, LANES), lambda i: (0, 0))],
        out_specs=[spec, spec],
        out_shape=[shape, shape],
        compiler_params=_cparams("parallel"),
        name="rope_tables",
    )(positions.reshape(m, 1), inv_lane)


def _conv_silu_body(x_ref, w_ref, b_ref, o_ref, *, k_blocks_from, k_scale):
    x = x_ref[0].astype(F32)
    w = w_ref[...]
    row = lax.broadcasted_iota(jnp.int32, x.shape, 0)
    y = x * w[M_CONV - 1:M_CONV, :] + b_ref[...]
    for d in range(1, M_CONV):
        xd = jnp.where(row >= d, pltpu.roll(x, d, 0), 0.0)
        y = y + xd * w[M_CONV - 1 - d:M_CONV - d, :]
    y = y * jax.nn.sigmoid(y)
    mul = jnp.where(pl.program_id(1) >= k_blocks_from, k_scale, 1.0).astype(F32)
    o_ref[0] = (y * mul).astype(o_ref.dtype)


def _conv_silu(proj3, conv_w, conv_b):
    b, s, _ = proj3.shape
    nblk = M_QK_W // LANES
    body = functools.partial(_conv_silu_body, k_blocks_from=nblk // 2, k_scale=M_DQK ** -0.5)
    return pl.pallas_call(
        body,
        grid=(b, nblk),
        in_specs=[pl.BlockSpec((1, s, LANES), lambda i, j: (i, 0, C_MQK // LANES + j)),
                  pl.BlockSpec((M_CONV, LANES), lambda i, j: (0, j)),
                  pl.BlockSpec((1, LANES), lambda i, j: (0, j))],
        out_specs=pl.BlockSpec((1, s, LANES), lambda i, j: (i, 0, j)),
        out_shape=jax.ShapeDtypeStruct((b, s, M_QK_W), BF16),
        compiler_params=_cparams("parallel", "arbitrary"),
        name="conv_silu",
    )(proj3, conv_w.astype(F32), conv_b.reshape(1, M_QK_W).astype(F32))


def _log_sigmoid(x):
    return jnp.minimum(x, 0.0) - jnp.log1p(jnp.exp(-jnp.abs(x)))


def _mlstm_body(qk_ref, v_ref, o_ref, gc_ref, gr_ref, bc_ref, br_ref, gn_ref, out_ref,
                c_ref, n_ref, m_ref, *, chunk):
    L = chunk

    @pl.when(pl.program_id(1) == 0)
    def _():
        c_ref[...] = jnp.zeros_like(c_ref)
        n_ref[...] = jnp.zeros_like(n_ref)
        m_ref[...] = jnp.zeros_like(m_ref)

    gc = gc_ref[0] + bc_ref[...]
    gr = gr_ref[0] + br_ref[...]
    row = lax.broadcasted_iota(jnp.int32, (L, L), 0)
    col = lax.broadcasted_iota(jnp.int32, (L, L), 1)
    causal = col <= row
    for h in range(M_HEADS):
        li_c = gc[:, h:h + 1]
        lf_c = _log_sigmoid(gc[:, M_HEADS + h:M_HEADS + h + 1])
        li_r = gr[h:h + 1, :]
        lf_r = _log_sigmoid(gr[M_HEADS + h:M_HEADS + h + 1, :])
        b_c = jnp.sum(jnp.where(causal, lf_r, 0.0), axis=1, keepdims=True)
        b_r = jnp.sum(jnp.where(row <= col, lf_c, 0.0), axis=0, keepdims=True)
        g = jnp.sum(lf_r, axis=1, keepdims=True)
        m_prev = m_ref[h][0:1, 0:1]
        dmat = jnp.where(causal, b_c - b_r + li_r, -jnp.inf)
        m_inter = b_c + m_prev
        m_t = jnp.maximum(jnp.max(dmat, axis=1, keepdims=True), m_inter)
        w_intra = jnp.exp(dmat - m_t)
        w_inter = jnp.exp(m_inter - m_t)
        q = qk_ref[0, :, h * M_DQK:(h + 1) * M_DQK]
        k = qk_ref[0, :, M_QK_W // 2 + h * M_DQK:M_QK_W // 2 + (h + 1) * M_DQK]
        v = v_ref[0, :, h * M_DV:(h + 1) * M_DV]
        s = lax.dot_general(q, k, (((1,), (1,)), ((), ())), preferred_element_type=F32) * w_intra
        c_old = c_ref[h]
        n_old = n_ref[h][0:1, :]
        num = (jnp.dot(s.astype(BF16), v, preferred_element_type=F32)
               + w_inter * jnp.dot(q, c_old.astype(BF16), preferred_element_type=F32))
        den = (jnp.sum(s, axis=1, keepdims=True)
               + w_inter * jnp.sum(q.astype(F32) * n_old, axis=1, keepdims=True))
        ht = num / jnp.maximum(jnp.abs(den), jnp.exp(-m_t))
        a_c = g - b_c + li_c
        m_new = jnp.maximum(g + m_prev, jnp.max(a_c, axis=0, keepdims=True))
        decay = jnp.exp(g + m_prev - m_new)
        kw = k.astype(F32) * jnp.exp(a_c - m_new)
        c_ref[h] = decay * c_old + lax.dot_general(
            kw.astype(BF16), v, (((0,), (0,)), ((), ())), preferred_element_type=F32)
        n_ref[h] = jnp.broadcast_to(decay * n_old + jnp.sum(kw, axis=0, keepdims=True), n_ref.shape[1:])
        m_ref[h] = jnp.broadcast_to(m_new, m_ref.shape[1:])
        hn = ht * lax.rsqrt(jnp.mean(ht * ht, axis=-1, keepdims=True) + EPS)
        hn = hn * gn_ref[:, h * M_DV:(h + 1) * M_DV]
        og = jax.nn.sigmoid(o_ref[0, :, h * M_DV:(h + 1) * M_DV].astype(F32))
        out_ref[0, :, h * M_DV:(h + 1) * M_DV] = (og * hn).astype(out_ref.dtype)


def _mlstm(qk, proj3, gates_c, gates_r, bias_c, bias_r, g_mhead, chunk):
    b, s, _ = proj3.shape
    L = chunk
    body = functools.partial(_mlstm_body, chunk=L)
    return pl.pallas_call(
        body,
        grid=(b, s // L),
        in_specs=[pl.BlockSpec((1, L, M_QK_W), lambda i, c: (i, c, 0)),
                  pl.BlockSpec((1, L, M_V_W), lambda i, c: (i, c, C_MV // M_V_W)),
                  pl.BlockSpec((1, L, M_V_W), lambda i, c: (i, c, C_MO // M_V_W)),
                  pl.BlockSpec((1, L, LANES), lambda i, c: (i, c, 0)),
                  pl.BlockSpec((1, 2 * M_HEADS, L), lambda i, c: (i, 0, c)),
                  pl.BlockSpec((1, LANES), lambda i, c: (0, 0)),
                  pl.BlockSpec((2 * M_HEADS, 1), lambda i, c: (0, 0)),
                  pl.BlockSpec((1, M_V_W), lambda i, c: (0, 0))],
        out_specs=pl.BlockSpec((1, L, M_V_W), lambda i, c: (i, c, 0)),
        out_shape=jax.ShapeDtypeStruct((b, s, M_V_W), BF16),
        scratch_shapes=[pltpu.VMEM((M_HEADS, M_DQK, M_DV), F32),
                        pltpu.VMEM((M_HEADS, 8, M_DQK), F32),
                        pltpu.VMEM((M_HEADS, 8, LANES), F32)],
        compiler_params=_cparams("parallel", "arbitrary"),
        name="mlstm",
    )(qk, proj3, proj3, gates_c, gates_r, bias_c, bias_r, g_mhead.reshape(1, M_V_W).astype(F32))


def _attn_body(q_ref, k_ref, v_ref, lam_ref, g_ref, o_ref, acc_ref, st_ref, pt_ref, *, lam_init):
    qi = pl.program_id(2)
    q = q_ref[0]
    tq = q.shape[0]
    tk = tq
    acc_ref[...] = jnp.zeros_like(acc_ref)
    init = ((jnp.full((1, tq), -jnp.inf, F32), jnp.zeros((1, tq), F32)),) * 2

    def scores(j, slot):
        k_blk = k_ref[0, pl.ds(pl.multiple_of(j * tk, tk), tk), :]
        for c in range(2):
            sl = slice(c * A_DH, (c + 1) * A_DH)
            st_ref[slot, c] = lax.dot_general(k_blk[:, sl], q[:, sl], (((1,), (1,)), ((), ())),
                                              preferred_element_type=F32)

    def softmax(slot, masked, stats):
        if masked:
            keep = (lax.broadcasted_iota(jnp.int32, (tk, tq), 0)
                    <= lax.broadcasted_iota(jnp.int32, (tk, tq), 1))
        new_stats, alphas = [], []
        for c in range(2):
            st = st_ref[slot, c]
            if masked:
                st = jnp.where(keep, st, -jnp.inf)
            m_old, l_old = stats[c]
            m_new = jnp.maximum(m_old, jnp.max(st, axis=0, keepdims=True))
            alpha = jnp.exp2(m_old - m_new)
            pt = jnp.exp2(st - m_new)
            pt_ref[slot, c] = pt.astype(BF16)
            new_stats.append((m_new, alpha * l_old + jnp.sum(pt, axis=0, keepdims=True)))
            alphas.append(alpha)
        return tuple(new_stats), tuple(alphas)

    def values(j, slot, alphas):
        v_blk = v_ref[0, pl.ds(pl.multiple_of(j * tk, tk), tk), :]
        for c in range(2):
            acc_ref[c] = alphas[c] * acc_ref[c] + lax.dot_general(
                v_blk, pt_ref[slot, c], (((0,), (0,)), ((), ())), preferred_element_type=F32)

    def tick(t, slot, carry):
        stats, alphas = carry
        scores(t, slot)
        carry = softmax(1 - slot, False, stats)
        values(t - 2, slot, alphas)
        return carry

    scores(0, 0)

    def one_block(stats):
        stats, alphas = softmax(0, True, stats)
        values(0, 0, alphas)
        return stats

    def drain(slot, carry):
        stats, alphas = carry
        values(qi - 1, 1 - slot, alphas)
        stats, alphas = softmax(slot, True, stats)
        values(qi, slot, alphas)
        return stats

    def several_blocks(stats):
        scores(1, 1)
        carry = softmax(0, False, stats)
        n_pairs = lax.shift_right_logical(qi - 1, 1)
        carry = lax.fori_loop(0, n_pairs, lambda u, cr: tick(3 + 2 * u, 1, tick(2 + 2 * u, 0, cr)), carry)
        qi_even = (qi & 1) == 0

        def unpaired_tick(carry):
            stats, alphas = carry
            values(qi - 2, 0, alphas)
            scores(qi, 0)
            return softmax(1, False, stats)

        carry = lax.cond(qi_even, unpaired_tick, lambda cr: cr, carry)
        return lax.cond(qi_even, functools.partial(drain, 0), functools.partial(drain, 1), carry)

    (_, l0), (_, l1) = lax.cond(qi >= 1, several_blocks, one_block, init)

    lv = lam_ref[...]
    lam = (jnp.exp(jnp.sum(lv[0:1] * lv[1:2], axis=1, keepdims=True))
           - jnp.exp(jnp.sum(lv[2:3] * lv[3:4], axis=1, keepdims=True)) + lam_init)
    o = acc_ref[0] / l0 - lam * (acc_ref[1] / l1)
    y = (o * lax.rsqrt(jnp.mean(o * o, axis=0, keepdims=True) + EPS)) * g_ref[...]
    o_ref[0] = (y * (1.0 - lam_init)).T.astype(o_ref.dtype)


def _diff_attention(proj3, lam_vecs, g_sub, lam_init, tq):
    b, s, _ = proj3.shape
    w = 2 * A_DH
    q0 = C_AQ // w
    k0 = C_AK // w
    v0 = C_AV // w
    return pl.pallas_call(
        functools.partial(_attn_body, lam_init=lam_init),
        grid=(b, A_HEADS, s // tq),
        in_specs=[pl.BlockSpec((1, tq, w), lambda i, h, qi: (i, qi, q0 + h)),
                  pl.BlockSpec((1, s, w), lambda i, h, qi: (i, 0, k0 + h)),
                  pl.BlockSpec((1, s, A_DV), lambda i, h, qi: (i, 0, v0 + h)),
                  pl.BlockSpec((4, A_DH), lambda i, h, qi: (0, 0)),
                  pl.BlockSpec((A_DV, 1), lambda i, h, qi: (0, 0))],
        out_specs=pl.BlockSpec((1, tq, A_DV), lambda i, h, qi: (i, qi, h)),
        out_shape=jax.ShapeDtypeStruct((b, s, A_V_W), BF16),
        scratch_shapes=[pltpu.VMEM((2, A_DV, tq), F32),
                        pltpu.VMEM((2, 2, tq, tq), F32),
                        pltpu.VMEM((2, 2, tq, tq), BF16)],
        compiler_params=_cparams("parallel", "parallel", "arbitrary"),
        name="diff_attention",
    )(proj3, proj3, proj3, lam_vecs, g_sub.reshape(A_DV, 1).astype(F32))


def kernel(x, positions, g_mix, w_in, conv_w, conv_b, i_bias, f_bias, g_mhead, lambda_q1, lambda_k1,
           lambda_q2, lambda_k2, g_sub, p_m, p_a, w_out, g_ffn, w_gate, w_up, w_down, g_final):
    B, S, D = x.shape
    M = B * S
    xf = x.reshape(M, D)
    tables = _rope_tables(positions)
    gate0 = C_AQ
    lane_to_dim = jnp.array(ROPE_LANE_TO_DIM, jnp.int32)
    chunk = _tile(S, 256)
    tq = _tile(S, 512)
    for l in range(DEPTH):
        lam_init = 0.8 - 0.6 * math.exp(-0.3 * l)
        w = w_in[l]
        qk0 = gate0 + 2 * M_HEADS
        w_qk = w[:, qk0:qk0 + 2 * A_Q_W].reshape(D, 2 * A_Q_W // A_DH, A_DH)[:, :, lane_to_dim]
        w_main = jnp.concatenate([w[:, :gate0], w_qk.reshape(D, 2 * A_Q_W), w[:, qk0 + 2 * A_Q_W:]],
                                 axis=1).astype(BF16)
        w_gates = jnp.pad(w[:, gate0:gate0 + 2 * M_HEADS], ((0, 0), (0, LANES - 2 * M_HEADS))).astype(BF16)

        proj, gates = _in_proj(xf, g_mix[l], w_main, w_gates, tables)
        proj3 = proj.reshape(B, S, PROJ_W)

        qk = _conv_silu(proj3, conv_w[l], conv_b[l])
        gates_c = gates.reshape(B, S, LANES)
        gates_r = jnp.swapaxes(gates_c[:, :, :2 * M_HEADS], 1, 2)
        bias8 = jnp.concatenate([i_bias[l], f_bias[l]]).astype(F32)
        bias_c = jnp.pad(bias8, (0, LANES - 2 * M_HEADS)).reshape(1, LANES)
        h_m = _mlstm(qk, proj3, gates_c, gates_r, bias_c, bias8.reshape(2 * M_HEADS, 1), g_mhead[l], chunk)

        lam_vecs = jnp.stack([lambda_q1[l], lambda_k1[l], lambda_q2[l], lambda_k2[l]]).astype(F32)
        h_a = _diff_attention(proj3, lam_vecs, g_sub[l], lam_init, tq)

        y = _gated_merge(h_m.reshape(M, M_V_W), h_a.reshape(M, A_V_W),
                         p_m[l].astype(BF16), p_a[l].astype(BF16), proj)
        xf = _matmul_residual(y, w_out[l].astype(BF16), xf, 1024, 1024, "out_proj")

        hh = _swiglu_up(xf, g_ffn[l], w_gate[l].astype(BF16), w_up[l].astype(BF16))
        xf = _matmul_residual(hh, w_down[l].astype(BF16), xf, 1024, 512, "ffn_down")
    return _rmsnorm(xf, g_final, x.dtype).reshape(B, S, D)
```

```python
import functools
import math

import jax
import jax.numpy as jnp
from jax import lax
from jax.experimental import pallas as pl
from jax.experimental.pallas import tpu as pltpu

DEPTH = 4
M_HEADS = 4
M_DQK = 128
M_DV = 256
M_CONV = 4
M_QK_W = 2 * M_HEADS * M_DQK
M_V_W = M_HEADS * M_DV
A_HEADS = 8
A_DH = 128
A_DV = 2 * A_DH
A_Q_W = A_HEADS * 2 * A_DH
A_V_W = A_HEADS * A_DV
ROPE_THETA = 500000.0
ROPE_DIM = A_DH // 4
EPS = 1e-6

LANES = 128
VMEM_LIMIT_BYTES = 60 * 1024 * 1024

C_MQK = 0
C_MV = C_MQK + M_QK_W
C_MO = C_MV + M_V_W
C_AQ = C_MO + M_V_W
C_AK = C_AQ + A_Q_W
C_AV = C_AK + A_Q_W
C_GM = C_AV + A_V_W
PROJ_W = C_GM + 2 * 2048

F32 = jnp.float32
BF16 = jnp.bfloat16


def _cparams(*sem, flags=None):
    return pltpu.CompilerParams(dimension_semantics=sem, vmem_limit_bytes=VMEM_LIMIT_BYTES, flags=flags)


def _tile(n, pref):
    t = min(n, pref)
    while n % t:
        t //= 2
    return t


def _rmsnorm_body(x_ref, g_ref, o_ref):
    x = x_ref[...].astype(F32)
    r = lax.rsqrt(jnp.mean(x * x, axis=-1, keepdims=True) + EPS)
    o_ref[...] = ((x * r) * g_ref[...]).astype(o_ref.dtype)


def _rmsnorm(x, g, out_dtype):
    m, d = x.shape
    tm = _tile(m, 1024)
    return pl.pallas_call(
        _rmsnorm_body,
        grid=(m // tm,),
        in_specs=[pl.BlockSpec((tm, d), lambda i: (i, 0)), pl.BlockSpec((1, d), lambda i: (0, 0))],
        out_specs=pl.BlockSpec((tm, d), lambda i: (i, 0)),
        out_shape=jax.ShapeDtypeStruct((m, d), out_dtype),
        compiler_params=_cparams("parallel"),
        name="rmsnorm",
    )(x, g.reshape(1, d).astype(F32))


def _normed_bf16(x_ref, g_ref):
    x = x_ref[...]
    r = lax.rsqrt(jnp.mean(x * x, axis=-1, keepdims=True) + EPS)
    return ((x * r) * g_ref[...]).astype(BF16)


def _in_proj_body(x_ref, g_ref, w_ref, wg_ref, cos_ref, sin_ref, o_ref, gates_ref, h_ref,
                  *, rope_lo, rope_q_hi, rope_hi, q_scale):
    j = pl.program_id(1)

    @pl.when(j == 0)
    def _():
        h = _normed_bf16(x_ref, g_ref)
        h_ref[...] = h
        gates_ref[...] = jnp.dot(h, wg_ref[...], preferred_element_type=F32)

    is_rope = (j >= rope_lo) & (j < rope_hi)

    @pl.when(jnp.logical_not(is_rope))
    def _():
        o_ref[...] = jnp.dot(h_ref[...], w_ref[...], preferred_element_type=F32).astype(o_ref.dtype)

    @pl.when(is_rope)
    def _():
        acc = jnp.dot(h_ref[...], w_ref[...], preferred_element_type=F32)
        mul = jnp.where(j < rope_q_hi, q_scale, 1.0).astype(F32)
        c = cos_ref[...] * mul
        s = sin_ref[...] * mul
        for hh in range(acc.shape[1] // LANES):
            sl = slice(hh * LANES, (hh + 1) * LANES)
            a = acc[:, sl]
            o_ref[:, sl] = (a * c + pltpu.roll(a, LANES // 2, 1) * s).astype(o_ref.dtype)


def _in_proj(x, g, w_main, w_gates, tables):
    m, d = x.shape
    n = w_main.shape[1]
    tm, tn = _tile(m, 1024), 1024
    assert C_AQ % tn == 0 and C_AK % tn == 0 and C_AV % tn == 0 and n % tn == 0
    tspec = pl.BlockSpec((tm, LANES), lambda i, j: (i, 0))
    body = functools.partial(_in_proj_body, rope_lo=C_AQ // tn, rope_q_hi=C_AK // tn, rope_hi=C_AV // tn,
                             q_scale=A_DH ** -0.5 * math.log2(math.e))
    return pl.pallas_call(
        body,
        grid=(m // tm, n // tn),
        in_specs=[pl.BlockSpec((tm, d), lambda i, j: (i, 0)),
                  pl.BlockSpec((1, d), lambda i, j: (0, 0)),
                  pl.BlockSpec((d, tn), lambda i, j: (0, j)),
                  pl.BlockSpec((d, LANES), lambda i, j: (0, 0)),
                  tspec, tspec],
        out_specs=[pl.BlockSpec((tm, tn), lambda i, j: (i, j)),
                   pl.BlockSpec((tm, LANES), lambda i, j: (i, 0))],
        out_shape=[jax.ShapeDtypeStruct((m, n), BF16), jax.ShapeDtypeStruct((m, LANES), F32)],
        scratch_shapes=[pltpu.VMEM((tm, d), BF16)],
        compiler_params=_cparams("parallel", "arbitrary"),
        name="in_proj",
    )(x, g.reshape(1, d).astype(F32), w_main, w_gates, *tables)


def _mm_res_body(a_ref, w_ref, x_ref, o_ref):
    o_ref[...] = x_ref[...] + jnp.dot(a_ref[...], w_ref[...], preferred_element_type=F32)


def _matmul_residual(a, w, x, tm_pref, tn_pref, name):
    m, k = a.shape
    n = w.shape[1]
    tm, tn = _tile(m, tm_pref), _tile(n, tn_pref)
    return pl.pallas_call(
        _mm_res_body,
        grid=(m // tm, n // tn),
        in_specs=[pl.BlockSpec((tm, k), lambda i, j: (i, 0)),
                  pl.BlockSpec((k, tn), lambda i, j: (0, j)),
                  pl.BlockSpec((tm, tn), lambda i, j: (i, j))],
        out_specs=pl.BlockSpec((tm, tn), lambda i, j: (i, j)),
        out_shape=jax.ShapeDtypeStruct((m, n), F32),
        input_output_aliases={2: 0},
        compiler_params=_cparams("parallel", "arbitrary"),
        name=name,
    )(a, w, x)


def _swiglu_body(x_ref, g_ref, wg_ref, wu_ref, o_ref, h_ref):
    @pl.when(pl.program_id(1) == 0)
    def _():
        h_ref[...] = _normed_bf16(x_ref, g_ref)

    h = h_ref[...]
    g = jnp.dot(h, wg_ref[...], preferred_element_type=F32)
    u = jnp.dot(h, wu_ref[...], preferred_element_type=F32)
    o_ref[...] = ((g * jax.nn.sigmoid(g)) * u).astype(o_ref.dtype)


def _swiglu_up(x, g, wg, wu):
    m, k = x.shape
    n = wg.shape[1]
    tm, tn = _tile(m, 1024), _tile(n, 512)
    return pl.pallas_call(
        _swiglu_body,
        grid=(m // tm, n // tn),
        in_specs=[pl.BlockSpec((tm, k), lambda i, j: (i, 0)),
                  pl.BlockSpec((1, k), lambda i, j: (0, 0)),
                  pl.BlockSpec((k, tn), lambda i, j: (0, j)),
                  pl.BlockSpec((k, tn), lambda i, j: (0, j))],
        out_specs=pl.BlockSpec((tm, tn), lambda i, j: (i, j)),
        out_shape=jax.ShapeDtypeStruct((m, n), BF16),
        scratch_shapes=[pltpu.VMEM((tm, k), BF16)],
        compiler_params=_cparams("parallel", "arbitrary"),
        name="swiglu_up",
    )(x, g.reshape(1, k).astype(F32), wg, wu)


def _merge_out_body(hm_ref, ha_ref, gm0_ref, gm1_ref, ga0_ref, ga1_ref, pm_ref, pa_ref, wo_ref, x_ref, o_ref):
    ym = jnp.dot(hm_ref[...], pm_ref[...], preferred_element_type=F32)
    ya = jnp.dot(ha_ref[...], pa_ref[...], preferred_element_type=F32)
    gm = jax.nn.sigmoid(jnp.concatenate([gm0_ref[...], gm1_ref[...]], axis=1).astype(F32))
    ga = jax.nn.sigmoid(jnp.concatenate([ga0_ref[...], ga1_ref[...]], axis=1).astype(F32))
    y = (gm * ym + ga * ya).astype(BF16)
    o_ref[...] = x_ref[...] + jnp.dot(y, wo_ref[...], preferred_element_type=F32)


def _merge_out(h_m, h_a, proj, p_m, p_a, w_out, x):
    m, d = x.shape
    tm = _tile(m, 512)
    gw = d // 2
    assert C_GM % gw == 0 and p_m.shape[1] == d and p_a.shape[1] == d
    g0 = C_GM // gw
    row = lambda w: pl.BlockSpec((tm, w), lambda i: (i, 0))
    gate = lambda jj: pl.BlockSpec((tm, gw), lambda i: (i, g0 + jj))
    resident = lambda a: pl.BlockSpec(a.shape, lambda i: (0, 0), pipeline_mode=pl.Buffered(1))
    return pl.pallas_call(
        _merge_out_body,
        grid=(m // tm,),
        in_specs=[row(h_m.shape[1]), row(h_a.shape[1]), gate(0), gate(1), gate(2), gate(3),
                  resident(p_m), resident(p_a), resident(w_out), row(d)],
        out_specs=row(d),
        out_shape=jax.ShapeDtypeStruct((m, d), F32),
        input_output_aliases={9: 0},
        compiler_params=_cparams("arbitrary"),
        name="merge_out",
    )(h_m, h_a, proj, proj, proj, proj, p_m, p_a, w_out, x)


_ROPE_HALF = ROPE_DIM // 2
ROPE_LANE_TO_DIM = (list(range(0, _ROPE_HALF)) + list(range(ROPE_DIM, LANES // 2 + _ROPE_HALF))
                    + list(range(_ROPE_HALF, ROPE_DIM)) + list(range(LANES // 2 + _ROPE_HALF, A_DH)))


def _rope_table_body(pos_ref, inv_ref, cos_ref, sin_ref):
    ang = pos_ref[...].astype(F32) * inv_ref[...]
    lane = lax.broadcasted_iota(jnp.int32, ang.shape, 1)
    s = jnp.sin(ang)
    cos_ref[...] = jnp.cos(ang)
    sin_ref[...] = jnp.where(lane < LANES // 2, -s, s)


def _rope_tables(positions):
    m = positions.size
    inv = jnp.power(ROPE_THETA, -jnp.arange(_ROPE_HALF, dtype=F32) * (2.0 / ROPE_DIM))
    inv_lane = (jnp.zeros((1, LANES), F32).at[0, :_ROPE_HALF].set(inv)
                .at[0, LANES // 2:LANES // 2 + _ROPE_HALF].set(inv))
    ts = _tile(m, 2048)
    spec = pl.BlockSpec((ts, LANES), lambda i: (i, 0))
    shape = jax.ShapeDtypeStruct((m, LANES), F32)
    return pl.pallas_call(
        _rope_table_body,
        grid=(m // ts,),
        in_specs=[pl.BlockSpec((ts, 1), lambda i: (i, 0)), pl.BlockSpec((1, LANES), lambda i: (0, 0))],
        out_specs=[spec, spec],
        out_shape=[shape, shape],
        compiler_params=_cparams("parallel"),
        name="rope_tables",
    )(positions.reshape(m, 1), inv_lane)


def _conv_silu_body(x_ref, w_ref, b_ref, o_ref, *, k_blocks_from, k_scale):
    x = x_ref[0].astype(F32)
    w = w_ref[...]
    row = lax.broadcasted_iota(jnp.int32, x.shape, 0)
    y = x * w[M_CONV - 1:M_CONV, :] + b_ref[...]
    for d in range(1, M_CONV):
        xd = jnp.where(row >= d, pltpu.roll(x, d, 0), 0.0)
        y = y + xd * w[M_CONV - 1 - d:M_CONV - d, :]
    y = y * jax.nn.sigmoid(y)
    mul = jnp.where(pl.program_id(1) >= k_blocks_from, k_scale, 1.0).astype(F32)
    o_ref[0] = (y * mul).astype(o_ref.dtype)


def _conv_silu(proj3, conv_w, conv_b):
    b, s, _ = proj3.shape
    nblk = M_QK_W // LANES
    body = functools.partial(_conv_silu_body, k_blocks_from=nblk // 2, k_scale=M_DQK ** -0.5)
    return pl.pallas_call(
        body,
        grid=(b, nblk),
        in_specs=[pl.BlockSpec((1, s, LANES), lambda i, j: (i, 0, C_MQK // LANES + j)),
                  pl.BlockSpec((M_CONV, LANES), lambda i, j: (0, j)),
                  pl.BlockSpec((1, LANES), lambda i, j: (0, j))],
        out_specs=pl.BlockSpec((1, s, LANES), lambda i, j: (i, 0, j)),
        out_shape=jax.ShapeDtypeStruct((b, s, M_QK_W), BF16),
        compiler_params=_cparams("parallel", "arbitrary"),
        name="conv_silu",
    )(proj3, conv_w.astype(F32), conv_b.reshape(1, M_QK_W).astype(F32))


def _log_sigmoid(x):
    return jnp.minimum(x, 0.0) - jnp.log1p(jnp.exp(-jnp.abs(x)))


def _mlstm_body(qk_ref, v_ref, o_ref, gc_ref, gr_ref, bc_ref, br_ref, gn_ref, out_ref,
                c_ref, n_ref, m_ref, *, chunk):
    L = chunk

    @pl.when(pl.program_id(1) == 0)
    def _():
        c_ref[...] = jnp.zeros_like(c_ref)
        n_ref[...] = jnp.zeros_like(n_ref)
        m_ref[...] = jnp.zeros_like(m_ref)

    gc = gc_ref[0] + bc_ref[...]
    gr = gr_ref[0] + br_ref[...]
    lf_all_c = _log_sigmoid(gc)
    lf_all_r = _log_sigmoid(gr)
    row = lax.broadcasted_iota(jnp.int32, (L, L), 0)
    col = lax.broadcasted_iota(jnp.int32, (L, L), 1)
    causal = col <= row
    for h in range(M_HEADS):
        li_c = gc[:, h:h + 1]
        lf_c = lf_all_c[:, M_HEADS + h:M_HEADS + h + 1]
        li_r = gr[h:h + 1, :]
        lf_r = lf_all_r[M_HEADS + h:M_HEADS + h + 1, :]
        b_c = jnp.sum(jnp.where(causal, lf_r, 0.0), axis=1, keepdims=True)
        b_r = jnp.sum(jnp.where(row <= col, lf_c, 0.0), axis=0, keepdims=True)
        g = jnp.sum(lf_r, axis=1, keepdims=True)
        m_prev = m_ref[h][0:1, 0:1]
        dmat = jnp.where(causal, b_c - b_r + li_r, -jnp.inf)
        m_inter = b_c + m_prev
        m_t = jnp.maximum(jnp.max(dmat, axis=1, keepdims=True), m_inter)
        w_intra = jnp.exp(dmat - m_t)
        w_inter = jnp.exp(m_inter - m_t)
        q = qk_ref[0, :, h * M_DQK:(h + 1) * M_DQK]
        k = qk_ref[0, :, M_QK_W // 2 + h * M_DQK:M_QK_W // 2 + (h + 1) * M_DQK]
        v = v_ref[0, :, h * M_DV:(h + 1) * M_DV]
        s = lax.dot_general(q, k, (((1,), (1,)), ((), ())), preferred_element_type=F32) * w_intra
        c_old = c_ref[h]
        n_old = n_ref[h][0:1, :]
        num = (jnp.dot(s.astype(BF16), v, preferred_element_type=F32)
               + w_inter * jnp.dot(q, c_old.astype(BF16), preferred_element_type=F32))
        den = (jnp.sum(s, axis=1, keepdims=True)
               + w_inter * jnp.sum(q.astype(F32) * n_old, axis=1, keepdims=True))
        ht = num / jnp.maximum(jnp.abs(den), jnp.exp(-m_t))
        a_c = g - b_c + li_c
        m_new = jnp.maximum(g + m_prev, jnp.max(a_c, axis=0, keepdims=True))
        decay = jnp.exp(g + m_prev - m_new)
        kw = k.astype(F32) * jnp.exp(a_c - m_new)
        c_ref[h] = decay * c_old + lax.dot_general(
            kw.astype(BF16), v, (((0,), (0,)), ((), ())), preferred_element_type=F32)
        n_ref[h] = jnp.broadcast_to(decay * n_old + jnp.sum(kw, axis=0, keepdims=True), n_ref.shape[1:])
        m_ref[h] = jnp.broadcast_to(m_new, m_ref.shape[1:])
        hn = ht * lax.rsqrt(jnp.mean(ht * ht, axis=-1, keepdims=True) + EPS)
        hn = hn * gn_ref[:, h * M_DV:(h + 1) * M_DV]
        og = jax.nn.sigmoid(o_ref[0, :, h * M_DV:(h + 1) * M_DV].astype(F32))
        out_ref[0, :, h * M_DV:(h + 1) * M_DV] = (og * hn).astype(out_ref.dtype)


def _mlstm(qk, proj3, gates_c, gates_r, bias_c, bias_r, g_mhead, chunk):
    b, s, _ = proj3.shape
    L = chunk
    body = functools.partial(_mlstm_body, chunk=L)
    return pl.pallas_call(
        body,
        grid=(b, s // L),
        in_specs=[pl.BlockSpec((1, L, M_QK_W), lambda i, c: (i, c, 0)),
                  pl.BlockSpec((1, L, M_V_W), lambda i, c: (i, c, C_MV // M_V_W)),
                  pl.BlockSpec((1, L, M_V_W), lambda i, c: (i, c, C_MO // M_V_W)),
                  pl.BlockSpec((1, L, LANES), lambda i, c: (i, c, 0)),
                  pl.BlockSpec((1, 2 * M_HEADS, L), lambda i, c: (i, 0, c)),
                  pl.BlockSpec((1, LANES), lambda i, c: (0, 0)),
                  pl.BlockSpec((2 * M_HEADS, 1), lambda i, c: (0, 0)),
                  pl.BlockSpec((1, M_V_W), lambda i, c: (0, 0))],
        out_specs=pl.BlockSpec((1, L, M_V_W), lambda i, c: (i, c, 0)),
        out_shape=jax.ShapeDtypeStruct((b, s, M_V_W), BF16),
        scratch_shapes=[pltpu.VMEM((M_HEADS, M_DQK, M_DV), F32),
                        pltpu.VMEM((M_HEADS, 8, M_DQK), F32),
                        pltpu.VMEM((M_HEADS, 8, LANES), F32)],
        compiler_params=_cparams("parallel", "arbitrary"),
        name="mlstm",
    )(qk, proj3, proj3, gates_c, gates_r, bias_c, bias_r, g_mhead.reshape(1, M_V_W).astype(F32))


def _attn_body(q_ref, k_ref, v_ref, lam_ref, g_ref, o_ref, acc_ref, st_ref, pt_ref, *, lam_init):
    qi = pl.program_id(2)
    q = q_ref[0]
    tq = q.shape[0]
    tk = tq
    acc_ref[...] = jnp.zeros_like(acc_ref)
    init = ((jnp.full((1, tq), -jnp.inf, F32), jnp.zeros((1, tq), F32)),) * 2

    def scores(j, slot):
        k_blk = k_ref[0, pl.ds(pl.multiple_of(j * tk, tk), tk), :]
        for c in range(2):
            sl = slice(c * A_DH, (c + 1) * A_DH)
            st_ref[slot, c] = lax.dot_general(k_blk[:, sl], q[:, sl], (((1,), (1,)), ((), ())),
                                              preferred_element_type=F32)

    def softmax(slot, masked, stats):
        if masked:
            keep = (lax.broadcasted_iota(jnp.int32, (tk, tq), 0)
                    <= lax.broadcasted_iota(jnp.int32, (tk, tq), 1))
        new_stats, alphas = [], []
        for c in range(2):
            st = st_ref[slot, c]
            if masked:
                st = jnp.where(keep, st, -jnp.inf)
            m_old, l_old = stats[c]
            m_new = jnp.maximum(m_old, jnp.max(st, axis=0, keepdims=True))
            alpha = jnp.exp2(m_old - m_new)
            pt = jnp.exp2(st - m_new)
            pt_ref[slot, c] = pt.astype(BF16)
            new_stats.append((m_new, alpha * l_old + jnp.sum(pt, axis=0, keepdims=True)))
            alphas.append(alpha)
        return tuple(new_stats), tuple(alphas)

    def values(j, slot, alphas):
        v_blk = v_ref[0, pl.ds(pl.multiple_of(j * tk, tk), tk), :]
        for c in range(2):
            acc_ref[c] = alphas[c] * acc_ref[c] + lax.dot_general(
                v_blk, pt_ref[slot, c], (((0,), (0,)), ((), ())), preferred_element_type=F32)

    def tick(t, slot, carry):
        stats, alphas = carry
        scores(t, slot)
        carry = softmax(1 - slot, False, stats)
        values(t - 2, slot, alphas)
        return carry

    def one_block(stats):
        scores(0, 0)
        stats, alphas = softmax(0, True, stats)
        values(0, 0, alphas)
        return stats

    def drain(slot, carry):
        stats, alphas = carry
        values(qi - 1, 1 - slot, alphas)
        stats, alphas = softmax(slot, True, stats)
        values(qi, slot, alphas)
        return stats

    def several_blocks(stats):
        scores(0, 0)
        scores(1, 1)
        carry = softmax(0, False, stats)
        n_pairs = lax.shift_right_logical(qi - 1, 1)
        carry = lax.fori_loop(0, n_pairs, lambda u, cr: tick(3 + 2 * u, 1, tick(2 + 2 * u, 0, cr)), carry)
        qi_even = (qi & 1) == 0

        def unpaired_tick(carry):
            stats, alphas = carry
            values(qi - 2, 0, alphas)
            scores(qi, 0)
            return softmax(1, False, stats)

        carry = lax.cond(qi_even, unpaired_tick, lambda cr: cr, carry)
        return lax.cond(qi_even, functools.partial(drain, 0), functools.partial(drain, 1), carry)

    (_, l0), (_, l1) = lax.cond(qi >= 1, several_blocks, one_block, init)

    lv = lam_ref[...]
    lam = (jnp.exp(jnp.sum(lv[0:1] * lv[1:2], axis=1, keepdims=True))
           - jnp.exp(jnp.sum(lv[2:3] * lv[3:4], axis=1, keepdims=True)) + lam_init)
    o = acc_ref[0] / l0 - lam * (acc_ref[1] / l1)
    y = (o * lax.rsqrt(jnp.mean(o * o, axis=0, keepdims=True) + EPS)) * g_ref[...]
    o_ref[0] = (y * (1.0 - lam_init)).T.astype(o_ref.dtype)


def _diff_attention(proj3, lam_vecs, g_sub, lam_init, tq):
    b, s, _ = proj3.shape
    w = 2 * A_DH
    q0 = C_AQ // w
    k0 = C_AK // w
    v0 = C_AV // w
    return pl.pallas_call(
        functools.partial(_attn_body, lam_init=lam_init),
        grid=(b, A_HEADS, s // tq),
        in_specs=[pl.BlockSpec((1, tq, w), lambda i, h, qi: (i, qi, q0 + h)),
                  pl.BlockSpec((1, s, w), lambda i, h, qi: (i, 0, k0 + h)),
                  pl.BlockSpec((1, s, A_DV), lambda i, h, qi: (i, 0, v0 + h)),
                  pl.BlockSpec((4, A_DH), lambda i, h, qi: (0, 0)),
                  pl.BlockSpec((A_DV, 1), lambda i, h, qi: (0, 0))],
        out_specs=pl.BlockSpec((1, tq, A_DV), lambda i, h, qi: (i, qi, h)),
        out_shape=jax.ShapeDtypeStruct((b, s, A_V_W), BF16),
        scratch_shapes=[pltpu.VMEM((2, A_DV, tq), F32),
                        pltpu.VMEM((2, 2, tq, tq), F32),
                        pltpu.VMEM((2, 2, tq, tq), BF16)],
        compiler_params=_cparams("parallel", "parallel", "arbitrary"),
        name="diff_attention",
    )(proj3, proj3, proj3, lam_vecs, g_sub.reshape(A_DV, 1).astype(F32))


def kernel(x, positions, g_mix, w_in, conv_w, conv_b, i_bias, f_bias, g_mhead, lambda_q1, lambda_k1,
           lambda_q2, lambda_k2, g_sub, p_m, p_a, w_out, g_ffn, w_gate, w_up, w_down, g_final):
    B, S, D = x.shape
    M = B * S
    xf = x.reshape(M, D)
    tables = _rope_tables(positions)
    gate0 = C_AQ
    lane_to_dim = jnp.array(ROPE_LANE_TO_DIM, jnp.int32)
    chunk = _tile(S, 256)
    tq = _tile(S, 512)
    for l in range(DEPTH):
        lam_init = 0.8 - 0.6 * math.exp(-0.3 * l)
        w = w_in[l]
        qk0 = gate0 + 2 * M_HEADS
        w_qk = w[:, qk0:qk0 + 2 * A_Q_W].reshape(D, 2 * A_Q_W // A_DH, A_DH)[:, :, lane_to_dim]
        w_main = jnp.concatenate([w[:, :gate0], w_qk.reshape(D, 2 * A_Q_W), w[:, qk0 + 2 * A_Q_W:]],
                                 axis=1).astype(BF16)
        w_gates = jnp.pad(w[:, gate0:gate0 + 2 * M_HEADS], ((0, 0), (0, LANES - 2 * M_HEADS))).astype(BF16)

        proj, gates = _in_proj(xf, g_mix[l], w_main, w_gates, tables)
        proj3 = proj.reshape(B, S, PROJ_W)

        qk = _conv_silu(proj3, conv_w[l], conv_b[l])
        gates_c = gates.reshape(B, S, LANES)
        gates_r = jnp.swapaxes(gates_c[:, :, :2 * M_HEADS], 1, 2)
        bias8 = jnp.concatenate([i_bias[l], f_bias[l]]).astype(F32)
        bias_c = jnp.pad(bias8, (0, LANES - 2 * M_HEADS)).reshape(1, LANES)
        h_m = _mlstm(qk, proj3, gates_c, gates_r, bias_c, bias8.reshape(2 * M_HEADS, 1), g_mhead[l], chunk)

        lam_vecs = jnp.stack([lambda_q1[l], lambda_k1[l], lambda_q2[l], lambda_k2[l]]).astype(F32)
        h_a = _diff_attention(proj3, lam_vecs, g_sub[l], lam_init, tq)

        xf = _merge_out(h_m.reshape(M, M_V_W), h_a.reshape(M, A_V_W), proj,
                        p_m[l].astype(BF16), p_a[l].astype(BF16), w_out[l].astype(BF16), xf)

        hh = _swiglu_up(xf, g_ffn[l], w_gate[l].astype(BF16), w_up[l].astype(BF16))
        xf = _matmul_residual(hh, w_down[l].astype(BF16), xf, 1024, 512, "ffn_down")
    return _rmsnorm(xf, g_final, x.dtype).reshape(B, S, D)
```

```python
import functools
import math

import jax
import jax.numpy as jnp
from jax import lax
from jax.experimental import pallas as pl
from jax.experimental.pallas import tpu as pltpu

DEPTH = 4
M_HEADS = 4
M_DQK = 128
M_DV = 256
M_CONV = 4
M_QK_W = 2 * M_HEADS * M_DQK
M_V_W = M_HEADS * M_DV
A_HEADS = 8
A_DH = 128
A_DV = 2 * A_DH
A_Q_W = A_HEADS * 2 * A_DH
A_V_W = A_HEADS * A_DV
ROPE_THETA = 500000.0
ROPE_DIM = A_DH // 4
EPS = 1e-6

LANES = 128
VMEM_LIMIT_BYTES = 60 * 1024 * 1024

C_MQK = 0
C_MV = C_MQK + M_QK_W
C_MO = C_MV + M_V_W
C_AQ = C_MO + M_V_W
C_AK = C_AQ + A_Q_W
C_AV = C_AK + A_Q_W
C_GM = C_AV + A_V_W
PROJ_W = C_GM + 2 * 2048

F32 = jnp.float32
BF16 = jnp.bfloat16


def _cparams(*sem, flags=None):
    return pltpu.CompilerParams(dimension_semantics=sem, vmem_limit_bytes=VMEM_LIMIT_BYTES, flags=flags)


def _tile(n, pref):
    t = min(n, pref)
    while n % t:
        t //= 2
    return t


def _normed_bf16(x_ref, g_ref):
    x = x_ref[...]
    r = lax.rsqrt(jnp.mean(x * x, axis=-1, keepdims=True) + EPS)
    return ((x * r) * g_ref[...]).astype(BF16)


def _in_proj_body(x_ref, g_ref, w_ref, wg_ref, cos_ref, sin_ref, o_ref, gates_ref, h_ref,
                  *, rope_lo, rope_q_hi, rope_hi, q_scale):
    j = pl.program_id(1)

    @pl.when(j == 0)
    def _():
        h = _normed_bf16(x_ref, g_ref)
        h_ref[...] = h
        gates_ref[...] = jnp.dot(h, wg_ref[...], preferred_element_type=F32)

    is_rope = (j >= rope_lo) & (j < rope_hi)

    @pl.when(jnp.logical_not(is_rope))
    def _():
        o_ref[...] = jnp.dot(h_ref[...], w_ref[...], preferred_element_type=F32).astype(o_ref.dtype)

    @pl.when(is_rope)
    def _():
        acc = jnp.dot(h_ref[...], w_ref[...], preferred_element_type=F32)
        mul = jnp.where(j < rope_q_hi, q_scale, 1.0).astype(F32)
        c = cos_ref[...] * mul
        s = sin_ref[...] * mul
        for hh in range(acc.shape[1] // LANES):
            sl = slice(hh * LANES, (hh + 1) * LANES)
            a = acc[:, sl]
            o_ref[:, sl] = (a * c + pltpu.roll(a, LANES // 2, 1) * s).astype(o_ref.dtype)


def _in_proj(x, g, w_main, w_gates, tables):
    m, d = x.shape
    n = w_main.shape[1]
    tm, tn = _tile(m, 1024), 1024
    assert C_AQ % tn == 0 and C_AK % tn == 0 and C_AV % tn == 0 and n % tn == 0
    tspec = pl.BlockSpec((tm, LANES), lambda i, j: (i, 0))
    body = functools.partial(_in_proj_body, rope_lo=C_AQ // tn, rope_q_hi=C_AK // tn, rope_hi=C_AV // tn,
                             q_scale=A_DH ** -0.5 * math.log2(math.e))
    return pl.pallas_call(
        body,
        grid=(m // tm, n // tn),
        in_specs=[pl.BlockSpec((tm, d), lambda i, j: (i, 0)),
                  pl.BlockSpec((1, d), lambda i, j: (0, 0)),
                  pl.BlockSpec((d, tn), lambda i, j: (0, j)),
                  pl.BlockSpec((d, LANES), lambda i, j: (0, 0)),
                  tspec, tspec],
        out_specs=[pl.BlockSpec((tm, tn), lambda i, j: (i, j)),
                   pl.BlockSpec((tm, LANES), lambda i, j: (i, 0))],
        out_shape=[jax.ShapeDtypeStruct((m, n), BF16), jax.ShapeDtypeStruct((m, LANES), F32)],
        scratch_shapes=[pltpu.VMEM((tm, d), BF16)],
        compiler_params=_cparams("parallel", "arbitrary"),
        name="in_proj",
    )(x, g.reshape(1, d).astype(F32), w_main, w_gates, *tables)


def _ffn_down_body(a_ref, w_ref, x_ref, o_ref):
    o_ref[...] = x_ref[...] + jnp.dot(a_ref[...], w_ref[...], preferred_element_type=F32)


def _ffn_down_final_body(a_ref, w_ref, x_ref, g_ref, o_ref):
    y = x_ref[...] + jnp.dot(a_ref[...], w_ref[...], preferred_element_type=F32)
    r = lax.rsqrt(jnp.mean(y * y, axis=-1, keepdims=True) + EPS)
    o_ref[...] = (y * r) * g_ref[...]


def _ffn_down(a, w, x, g_final=None):
    m, k = a.shape
    n = w.shape[1]
    tm = _tile(m, 512)
    row = lambda width: pl.BlockSpec((tm, width), lambda i: (i, 0))
    in_specs = [row(k), pl.BlockSpec((k, n), lambda i: (0, 0), pipeline_mode=pl.Buffered(1)), row(n)]
    args = [a, w, x]
    body = _ffn_down_body
    if g_final is not None:
        in_specs.append(pl.BlockSpec((1, n), lambda i: (0, 0)))
        args.append(g_final.reshape(1, n).astype(F32))
        body = _ffn_down_final_body
    return pl.pallas_call(
        body,
        grid=(m // tm,),
        in_specs=in_specs,
        out_specs=row(n),
        out_shape=jax.ShapeDtypeStruct((m, n), F32),
        input_output_aliases={2: 0},
        compiler_params=_cparams("arbitrary"),
        name="ffn_down",
    )(*args)


def _swiglu_body(x_ref, g_ref, wg_ref, wu_ref, o_ref, h_ref):
    @pl.when(pl.program_id(1) == 0)
    def _():
        h_ref[...] = _normed_bf16(x_ref, g_ref)

    h = h_ref[...]
    g = jnp.dot(h, wg_ref[...], preferred_element_type=F32)
    u = jnp.dot(h, wu_ref[...], preferred_element_type=F32)
    o_ref[...] = ((g * jax.nn.sigmoid(g)) * u).astype(o_ref.dtype)


def _swiglu_up(x, g, wg, wu):
    m, k = x.shape
    n = wg.shape[1]
    tm, tn = _tile(m, 1024), _tile(n, 512)
    return pl.pallas_call(
        _swiglu_body,
        grid=(m // tm, n // tn),
        in_specs=[pl.BlockSpec((tm, k), lambda i, j: (i, 0)),
                  pl.BlockSpec((1, k), lambda i, j: (0, 0)),
                  pl.BlockSpec((k, tn), lambda i, j: (0, j)),
                  pl.BlockSpec((k, tn), lambda i, j: (0, j))],
        out_specs=pl.BlockSpec((tm, tn), lambda i, j: (i, j)),
        out_shape=jax.ShapeDtypeStruct((m, n), BF16),
        scratch_shapes=[pltpu.VMEM((tm, k), BF16)],
        compiler_params=_cparams("parallel", "arbitrary"),
        name="swiglu_up",
    )(x, g.reshape(1, k).astype(F32), wg, wu)


def _merge_out_body(hm_ref, ha_ref, gm0_ref, gm1_ref, ga0_ref, ga1_ref, pm_ref, pa_ref, wo_ref, x_ref, o_ref):
    ym = jnp.dot(hm_ref[...], pm_ref[...], preferred_element_type=F32)
    ya = jnp.dot(ha_ref[...], pa_ref[...], preferred_element_type=F32)
    gm = jax.nn.sigmoid(jnp.concatenate([gm0_ref[...], gm1_ref[...]], axis=1).astype(F32))
    ga = jax.nn.sigmoid(jnp.concatenate([ga0_ref[...], ga1_ref[...]], axis=1).astype(F32))
    y = (gm * ym + ga * ya).astype(BF16)
    o_ref[...] = x_ref[...] + jnp.dot(y, wo_ref[...], preferred_element_type=F32)


def _merge_out(h_m, h_a, proj, p_m, p_a, w_out, x, in_place):
    m, d = x.shape
    tm = _tile(m, 512)
    gw = d // 2
    assert C_GM % gw == 0 and p_m.shape[1] == d and p_a.shape[1] == d
    g0 = C_GM // gw
    row = lambda w: pl.BlockSpec((tm, w), lambda i: (i, 0))
    gate = lambda jj: pl.BlockSpec((tm, gw), lambda i: (i, g0 + jj))
    resident = lambda a: pl.BlockSpec(a.shape, lambda i: (0, 0), pipeline_mode=pl.Buffered(1))
    return pl.pallas_call(
        _merge_out_body,
        grid=(m // tm,),
        in_specs=[row(h_m.shape[1]), row(h_a.shape[1]), gate(0), gate(1), gate(2), gate(3),
                  resident(p_m), resident(p_a), resident(w_out), row(d)],
        out_specs=row(d),
        out_shape=jax.ShapeDtypeStruct((m, d), F32),
        input_output_aliases={9: 0} if in_place else {},
        compiler_params=_cparams("arbitrary"),
        name="merge_out",
    )(h_m, h_a, proj, proj, proj, proj, p_m, p_a, w_out, x)


_ROPE_HALF = ROPE_DIM // 2
ROPE_LANE_TO_DIM = (list(range(0, _ROPE_HALF)) + list(range(ROPE_DIM, LANES // 2 + _ROPE_HALF))
                    + list(range(_ROPE_HALF, ROPE_DIM)) + list(range(LANES // 2 + _ROPE_HALF, A_DH)))


def _rope_table_body(pos_ref, inv_ref, cos_ref, sin_ref):
    ang = pos_ref[...].astype(F32) * inv_ref[...]
    lane = lax.broadcasted_iota(jnp.int32, ang.shape, 1)
    s = jnp.sin(ang)
    cos_ref[...] = jnp.cos(ang)
    sin_ref[...] = jnp.where(lane < LANES // 2, -s, s)


def _rope_tables(positions):
    m = positions.size
    inv = jnp.power(ROPE_THETA, -jnp.arange(_ROPE_HALF, dtype=F32) * (2.0 / ROPE_DIM))
    inv_lane = (jnp.zeros((1, LANES), F32).at[0, :_ROPE_HALF].set(inv)
                .at[0, LANES // 2:LANES // 2 + _ROPE_HALF].set(inv))
    ts = _tile(m, 2048)
    spec = pl.BlockSpec((ts, LANES), lambda i: (i, 0))
    shape = jax.ShapeDtypeStruct((m, LANES), F32)
    return pl.pallas_call(
        _rope_table_body,
        grid=(m // ts,),
        in_specs=[pl.BlockSpec((ts, 1), lambda i: (i, 0)), pl.BlockSpec((1, LANES), lambda i: (0, 0))],
        out_specs=[spec, spec],
        out_shape=[shape, shape],
        compiler_params=_cparams("parallel"),
        name="rope_tables",
    )(positions.reshape(m, 1), inv_lane)


def _conv_silu_body(x_ref, w_ref, b_ref, o_ref, *, k_blocks_from, k_scale):
    x = x_ref[0].astype(F32)
    w = w_ref[...]
    row = lax.broadcasted_iota(jnp.int32, x.shape, 0)
    y = x * w[M_CONV - 1:M_CONV, :] + b_ref[...]
    for d in range(1, M_CONV):
        xd = jnp.where(row >= d, pltpu.roll(x, d, 0), 0.0)
        y = y + xd * w[M_CONV - 1 - d:M_CONV - d, :]
    y = y * jax.nn.sigmoid(y)
    mul = jnp.where(pl.program_id(1) >= k_blocks_from, k_scale, 1.0).astype(F32)
    o_ref[0] = (y * mul).astype(o_ref.dtype)


def _conv_silu(proj3, conv_w, conv_b):
    b, s, _ = proj3.shape
    nblk = M_QK_W // LANES
    body = functools.partial(_conv_silu_body, k_blocks_from=nblk // 2, k_scale=M_DQK ** -0.5)
    return pl.pallas_call(
        body,
        grid=(b, nblk),
        in_specs=[pl.BlockSpec((1, s, LANES), lambda i, j: (i, 0, C_MQK // LANES + j)),
                  pl.BlockSpec((M_CONV, LANES), lambda i, j: (0, j)),
                  pl.BlockSpec((1, LANES), lambda i, j: (0, j))],
        out_specs=pl.BlockSpec((1, s, LANES), lambda i, j: (i, 0, j)),
        out_shape=jax.ShapeDtypeStruct((b, s, M_QK_W), BF16),
        compiler_params=_cparams("parallel", "arbitrary"),
        name="conv_silu",
    )(proj3, conv_w.astype(F32), conv_b.reshape(1, M_QK_W).astype(F32))


def _log_sigmoid(x):
    return jnp.minimum(x, 0.0) - jnp.log1p(jnp.exp(-jnp.abs(x)))


def _mlstm_body(qk_ref, v_ref, o_ref, gc_ref, gr_ref, bc_ref, br_ref, gn_ref, out_ref,
                c_ref, n_ref, m_ref, *, chunk):
    L = chunk

    @pl.when(pl.program_id(1) == 0)
    def _():
        c_ref[...] = jnp.zeros_like(c_ref)
        n_ref[...] = jnp.zeros_like(n_ref)
        m_ref[...] = jnp.zeros_like(m_ref)

    gc = gc_ref[0] + bc_ref[...]
    gr = gr_ref[0] + br_ref[...]
    lf_all_c = _log_sigmoid(gc)
    lf_all_r = _log_sigmoid(gr)
    row = lax.broadcasted_iota(jnp.int32, (L, L), 0)
    col = lax.broadcasted_iota(jnp.int32, (L, L), 1)
    causal = col <= row
    for h in range(M_HEADS):
        li_c = gc[:, h:h + 1]
        lf_c = lf_all_c[:, M_HEADS + h:M_HEADS + h + 1]
        li_r = gr[h:h + 1, :]
        lf_r = lf_all_r[M_HEADS + h:M_HEADS + h + 1, :]
        b_c = jnp.sum(jnp.where(causal, lf_r, 0.0), axis=1, keepdims=True)
        b_r = jnp.sum(jnp.where(row <= col, lf_c, 0.0), axis=0, keepdims=True)
        g = jnp.sum(lf_r, axis=1, keepdims=True)
        m_prev = m_ref[h][0:1, 0:1]
        dmat = jnp.where(causal, b_c - b_r + li_r, -jnp.inf)
        m_inter = b_c + m_prev
        m_t = jnp.maximum(jnp.max(dmat, axis=1, keepdims=True), m_inter)
        w_intra = jnp.exp(dmat - m_t)
        w_inter = jnp.exp(m_inter - m_t)
        q = qk_ref[0, :, h * M_DQK:(h + 1) * M_DQK]
        k = qk_ref[0, :, M_QK_W // 2 + h * M_DQK:M_QK_W // 2 + (h + 1) * M_DQK]
        v = v_ref[0, :, h * M_DV:(h + 1) * M_DV]
        s = lax.dot_general(q, k, (((1,), (1,)), ((), ())), preferred_element_type=F32) * w_intra
        c_old = c_ref[h]
        n_old = n_ref[h][0:1, :]
        num = (jnp.dot(s.astype(BF16), v, preferred_element_type=F32)
               + w_inter * jnp.dot(q, c_old.astype(BF16), preferred_element_type=F32))
        den = (jnp.sum(s, axis=1, keepdims=True)
               + w_inter * jnp.sum(q.astype(F32) * n_old, axis=1, keepdims=True))
        ht = num / jnp.maximum(jnp.abs(den), jnp.exp(-m_t))
        a_c = g - b_c + li_c
        m_new = jnp.maximum(g + m_prev, jnp.max(a_c, axis=0, keepdims=True))
        decay = jnp.exp(g + m_prev - m_new)
        kw = k.astype(F32) * jnp.exp(a_c - m_new)
        c_ref[h] = decay * c_old + lax.dot_general(
            kw.astype(BF16), v, (((0,), (0,)), ((), ())), preferred_element_type=F32)
        n_ref[h] = jnp.broadcast_to(decay * n_old + jnp.sum(kw, axis=0, keepdims=True), n_ref.shape[1:])
        m_ref[h] = jnp.broadcast_to(m_new, m_ref.shape[1:])
        hn = ht * lax.rsqrt(jnp.mean(ht * ht, axis=-1, keepdims=True) + EPS)
        hn = hn * gn_ref[:, h * M_DV:(h + 1) * M_DV]
        og = jax.nn.sigmoid(o_ref[0, :, h * M_DV:(h + 1) * M_DV].astype(F32))
        out_ref[0, :, h * M_DV:(h + 1) * M_DV] = (og * hn).astype(out_ref.dtype)


def _mlstm(qk, proj3, gates_c, gates_r, bias_c, bias_r, g_mhead, chunk):
    b, s, _ = proj3.shape
    L = chunk
    body = functools.partial(_mlstm_body, chunk=L)
    return pl.pallas_call(
        body,
        grid=(b, s // L),
        in_specs=[pl.BlockSpec((1, L, M_QK_W), lambda i, c: (i, c, 0)),
                  pl.BlockSpec((1, L, M_V_W), lambda i, c: (i, c, C_MV // M_V_W)),
                  pl.BlockSpec((1, L, M_V_W), lambda i, c: (i, c, C_MO // M_V_W)),
                  pl.BlockSpec((1, L, LANES), lambda i, c: (i, c, 0)),
                  pl.BlockSpec((1, 2 * M_HEADS, L), lambda i, c: (i, 0, c)),
                  pl.BlockSpec((1, LANES), lambda i, c: (0, 0)),
                  pl.BlockSpec((2 * M_HEADS, 1), lambda i, c: (0, 0)),
                  pl.BlockSpec((1, M_V_W), lambda i, c: (0, 0))],
        out_specs=pl.BlockSpec((1, L, M_V_W), lambda i, c: (i, c, 0)),
        out_shape=jax.ShapeDtypeStruct((b, s, M_V_W), BF16),
        scratch_shapes=[pltpu.VMEM((M_HEADS, M_DQK, M_DV), F32),
                        pltpu.VMEM((M_HEADS, 8, M_DQK), F32),
                        pltpu.VMEM((M_HEADS, 8, LANES), F32)],
        compiler_params=_cparams("parallel", "arbitrary"),
        name="mlstm",
    )(qk, proj3, proj3, gates_c, gates_r, bias_c, bias_r, g_mhead.reshape(1, M_V_W).astype(F32))


def _attn_body(q_ref, k_ref, v_ref, lam_ref, g_ref, o_ref, acc_ref, st_ref, pt_ref, *, lam_init):
    qi = pl.program_id(2)
    q = q_ref[0]
    tq = q.shape[0]
    tk = tq
    acc_ref[...] = jnp.zeros_like(acc_ref)
    init = ((jnp.full((1, tq), -jnp.inf, F32), jnp.zeros((1, tq), F32)),) * 2

    def scores(j, slot):
        k_blk = k_ref[0, pl.ds(pl.multiple_of(j * tk, tk), tk), :]
        for c in range(2):
            sl = slice(c * A_DH, (c + 1) * A_DH)
            st_ref[slot, c] = lax.dot_general(k_blk[:, sl], q[:, sl], (((1,), (1,)), ((), ())),
                                              preferred_element_type=F32)

    def softmax(slot, masked, stats):
        if masked:
            keep = (lax.broadcasted_iota(jnp.int32, (tk, tq), 0)
                    <= lax.broadcasted_iota(jnp.int32, (tk, tq), 1))
        new_stats, alphas = [], []
        for c in range(2):
            st = st_ref[slot, c]
            if masked:
                st = jnp.where(keep, st, -jnp.inf)
            m_old, l_old = stats[c]
            m_new = jnp.maximum(m_old, jnp.max(st, axis=0, keepdims=True))
            alpha = jnp.exp2(m_old - m_new)
            pt = jnp.exp2(st - m_new)
            pt_ref[slot, c] = pt.astype(BF16)
            new_stats.append((m_new, alpha * l_old + jnp.sum(pt, axis=0, keepdims=True)))
            alphas.append(alpha)
        return tuple(new_stats), tuple(alphas)

    def values(j, slot, alphas):
        v_blk = v_ref[0, pl.ds(pl.multiple_of(j * tk, tk), tk), :]
        for c in range(2):
            acc_ref[c] = alphas[c] * acc_ref[c] + lax.dot_general(
                v_blk, pt_ref[slot, c], (((0,), (0,)), ((), ())), preferred_element_type=F32)

    def tick(t, slot, carry):
        stats, alphas = carry
        scores(t, slot)
        carry = softmax(1 - slot, False, stats)
        values(t - 2, slot, alphas)
        return carry

    def one_block(stats):
        scores(0, 0)
        stats, alphas = softmax(0, True, stats)
        values(0, 0, alphas)
        return stats

    def drain(slot, carry):
        stats, alphas = carry
        values(qi - 1, 1 - slot, alphas)
        stats, alphas = softmax(slot, True, stats)
        values(qi, slot, alphas)
        return stats

    def several_blocks(stats):
        scores(0, 0)
        scores(1, 1)
        carry = softmax(0, False, stats)
        n_pairs = lax.shift_right_logical(qi - 1, 1)
        carry = lax.fori_loop(0, n_pairs, lambda u, cr: tick(3 + 2 * u, 1, tick(2 + 2 * u, 0, cr)), carry)
        qi_even = (qi & 1) == 0

        def unpaired_tick(carry):
            stats, alphas = carry
            values(qi - 2, 0, alphas)
            scores(qi, 0)
            return softmax(1, False, stats)

        carry = lax.cond(qi_even, unpaired_tick, lambda cr: cr, carry)
        return lax.cond(qi_even, functools.partial(drain, 0), functools.partial(drain, 1), carry)

    (_, l0), (_, l1) = lax.cond(qi >= 1, several_blocks, one_block, init)

    lv = lam_ref[...]
    lam = (jnp.exp(jnp.sum(lv[0:1] * lv[1:2], axis=1, keepdims=True))
           - jnp.exp(jnp.sum(lv[2:3] * lv[3:4], axis=1, keepdims=True)) + lam_init)
    o = acc_ref[0] / l0 - lam * (acc_ref[1] / l1)
    y = (o * lax.rsqrt(jnp.mean(o * o, axis=0, keepdims=True) + EPS)) * g_ref[...]
    o_ref[0] = (y * (1.0 - lam_init)).T.astype(o_ref.dtype)


def _diff_attention(proj3, lam_vecs, g_sub, lam_init, tq):
    b, s, _ = proj3.shape
    w = 2 * A_DH
    q0 = C_AQ // w
    k0 = C_AK // w
    v0 = C_AV // w
    return pl.pallas_call(
        functools.partial(_attn_body, lam_init=lam_init),
        grid=(b, A_HEADS, s // tq),
        in_specs=[pl.BlockSpec((1, tq, w), lambda i, h, qi: (i, qi, q0 + h)),
                  pl.BlockSpec((1, s, w), lambda i, h, qi: (i, 0, k0 + h)),
                  pl.BlockSpec((1, s, A_DV), lambda i, h, qi: (i, 0, v0 + h)),
                  pl.BlockSpec((4, A_DH), lambda i, h, qi: (0, 0)),
                  pl.BlockSpec((A_DV, 1), lambda i, h, qi: (0, 0))],
        out_specs=pl.BlockSpec((1, tq, A_DV), lambda i, h, qi: (i, qi, h)),
        out_shape=jax.ShapeDtypeStruct((b, s, A_V_W), BF16),
        scratch_shapes=[pltpu.VMEM((2, A_DV, tq), F32),
                        pltpu.VMEM((2, 2, tq, tq), F32),
                        pltpu.VMEM((2, 2, tq, tq), BF16)],
        compiler_params=_cparams("parallel", "parallel", "arbitrary"),
        name="diff_attention",
    )(proj3, proj3, proj3, lam_vecs, g_sub.reshape(A_DV, 1).astype(F32))


def kernel(x, positions, g_mix, w_in, conv_w, conv_b, i_bias, f_bias, g_mhead, lambda_q1, lambda_k1,
           lambda_q2, lambda_k2, g_sub, p_m, p_a, w_out, g_ffn, w_gate, w_up, w_down, g_final):
    B, S, D = x.shape
    M = B * S
    xf = x.reshape(M, D)
    tables = _rope_tables(positions)
    gate0 = C_AQ
    lane_to_dim = jnp.array(ROPE_LANE_TO_DIM, jnp.int32)
    chunk = _tile(S, 256)
    tq = _tile(S, 512)
    w_in_bf16 = w_in.astype(BF16)
    for l in range(DEPTH):
        lam_init = 0.8 - 0.6 * math.exp(-0.3 * l)
        w = w_in_bf16[l]
        qk0 = gate0 + 2 * M_HEADS
        w_qk = w[:, qk0:qk0 + 2 * A_Q_W].reshape(D, 2 * A_Q_W // A_DH, A_DH)[:, :, lane_to_dim]
        w_main = jnp.concatenate([w[:, :gate0], w_qk.reshape(D, 2 * A_Q_W), w[:, qk0 + 2 * A_Q_W:]], axis=1)
        w_gates = jnp.pad(w[:, gate0:gate0 + 2 * M_HEADS], ((0, 0), (0, LANES - 2 * M_HEADS)))

        proj, gates = _in_proj(xf, g_mix[l], w_main, w_gates, tables)
        proj3 = proj.reshape(B, S, PROJ_W)

        qk = _conv_silu(proj3, conv_w[l], conv_b[l])
        gates_c = gates.reshape(B, S, LANES)
        gates_r = jnp.swapaxes(gates_c[:, :, :2 * M_HEADS], 1, 2)
        bias8 = jnp.concatenate([i_bias[l], f_bias[l]]).astype(F32)
        bias_c = jnp.pad(bias8, (0, LANES - 2 * M_HEADS)).reshape(1, LANES)
        h_m = _mlstm(qk, proj3, gates_c, gates_r, bias_c, bias8.reshape(2 * M_HEADS, 1), g_mhead[l], chunk)

        lam_vecs = jnp.stack([lambda_q1[l], lambda_k1[l], lambda_q2[l], lambda_k2[l]]).astype(F32)
        h_a = _diff_attention(proj3, lam_vecs, g_sub[l], lam_init, tq)

        xf = _merge_out(h_m.reshape(M, M_V_W), h_a.reshape(M, A_V_W), proj,
                        p_m[l].astype(BF16), p_a[l].astype(BF16), w_out[l].astype(BF16), xf, in_place=l > 0)

        hh = _swiglu_up(xf, g_ffn[l], w_gate[l].astype(BF16), w_up[l].astype(BF16))
        xf = _ffn_down(hh, w_down[l].astype(BF16), xf, g_final if l == DEPTH - 1 else None)
    return xf.astype(x.dtype).reshape(B, S, D)
```

```python
import functools
import math

import jax
import jax.numpy as jnp
from jax import lax
from jax.experimental import pallas as pl
from jax.experimental.pallas import tpu as pltpu

DEPTH = 4
M_HEADS = 4
M_DQK = 128
M_DV = 256
M_CONV = 4
M_QK_W = 2 * M_HEADS * M_DQK
M_V_W = M_HEADS * M_DV
A_HEADS = 8
A_DH = 128
A_DV = 2 * A_DH
A_Q_W = A_HEADS * 2 * A_DH
A_V_W = A_HEADS * A_DV
ROPE_THETA = 500000.0
ROPE_DIM = A_DH // 4
EPS = 1e-6

LANES = 128
VMEM_LIMIT_BYTES = 60 * 1024 * 1024

C_MQK = 0
C_MV = C_MQK + M_QK_W
C_MO = C_MV + M_V_W
C_AQ = C_MO + M_V_W
C_AK = C_AQ + A_Q_W
C_AV = C_AK + A_Q_W
C_GM = C_AV + A_V_W
PROJ_W = C_GM + 2 * 2048

F32 = jnp.float32
BF16 = jnp.bfloat16


def _cparams(*sem, flags=None):
    return pltpu.CompilerParams(dimension_semantics=sem, vmem_limit_bytes=VMEM_LIMIT_BYTES, flags=flags)


def _tile(n, pref):
    t = min(n, pref)
    while n % t:
        t //= 2
    return t


def _normed_bf16(x_ref, g_ref):
    x = x_ref[...]
    r = lax.rsqrt(jnp.mean(x * x, axis=-1, keepdims=True) + EPS)
    return ((x * r) * g_ref[...]).astype(BF16)


def _in_proj_body(x_ref, g_ref, w_ref, wg_ref, cos_ref, sin_ref, o_ref, gates_ref, h_ref,
                  *, rope_lo, rope_q_hi, rope_hi, q_scale):
    j = pl.program_id(1)

    @pl.when(j == 0)
    def _():
        h = _normed_bf16(x_ref, g_ref)
        h_ref[...] = h
        gates_ref[...] = jnp.dot(h, wg_ref[...], preferred_element_type=F32)

    is_rope = (j >= rope_lo) & (j < rope_hi)

    @pl.when(jnp.logical_not(is_rope))
    def _():
        o_ref[...] = jnp.dot(h_ref[...], w_ref[...], preferred_element_type=F32).astype(o_ref.dtype)

    @pl.when(is_rope)
    def _():
        acc = jnp.dot(h_ref[...], w_ref[...], preferred_element_type=F32)
        mul = jnp.where(j < rope_q_hi, q_scale, 1.0).astype(F32)
        c = cos_ref[...] * mul
        s = sin_ref[...] * mul
        for hh in range(acc.shape[1] // LANES):
            sl = slice(hh * LANES, (hh + 1) * LANES)
            a = acc[:, sl]
            o_ref[:, sl] = (a * c + pltpu.roll(a, LANES // 2, 1) * s).astype(o_ref.dtype)


def _in_proj(x, g, w_main, w_gates, tables):
    m, d = x.shape
    n = w_main.shape[1]
    tm, tn = _tile(m, 1024), 1024
    assert C_AQ % tn == 0 and C_AK % tn == 0 and C_AV % tn == 0 and n % tn == 0
    tspec = pl.BlockSpec((tm, LANES), lambda i, j: (i, 0))
    body = functools.partial(_in_proj_body, rope_lo=C_AQ // tn, rope_q_hi=C_AK // tn, rope_hi=C_AV // tn,
                             q_scale=A_DH ** -0.5 * math.log2(math.e))
    return pl.pallas_call(
        body,
        grid=(m // tm, n // tn),
        in_specs=[pl.BlockSpec((tm, d), lambda i, j: (i, 0)),
                  pl.BlockSpec((1, d), lambda i, j: (0, 0)),
                  pl.BlockSpec((d, tn), lambda i, j: (0, j)),
                  pl.BlockSpec((d, LANES), lambda i, j: (0, 0)),
                  tspec, tspec],
        out_specs=[pl.BlockSpec((tm, tn), lambda i, j: (i, j)),
                   pl.BlockSpec((tm, LANES), lambda i, j: (i, 0))],
        out_shape=[jax.ShapeDtypeStruct((m, n), BF16), jax.ShapeDtypeStruct((m, LANES), F32)],
        scratch_shapes=[pltpu.VMEM((tm, d), BF16)],
        compiler_params=_cparams("parallel", "arbitrary"),
        name="in_proj",
    )(x, g.reshape(1, d).astype(F32), w_main, w_gates, *tables)


def _ffn_down_body(a_ref, w_ref, x_ref, o_ref):
    o_ref[...] = x_ref[...] + jnp.dot(a_ref[...], w_ref[...], preferred_element_type=F32)


def _ffn_down_final_body(a_ref, w_ref, x_ref, g_ref, o_ref):
    y = x_ref[...] + jnp.dot(a_ref[...], w_ref[...], preferred_element_type=F32)
    r = lax.rsqrt(jnp.mean(y * y, axis=-1, keepdims=True) + EPS)
    o_ref[...] = (y * r) * g_ref[...]


def _ffn_down(a, w, x, g_final=None):
    m, k = a.shape
    n = w.shape[1]
    tm = _tile(m, 512)
    row = lambda width: pl.BlockSpec((tm, width), lambda i: (i, 0))
    in_specs = [row(k), pl.BlockSpec((k, n), lambda i: (0, 0), pipeline_mode=pl.Buffered(1)), row(n)]
    args = [a, w, x]
    body = _ffn_down_body
    if g_final is not None:
        in_specs.append(pl.BlockSpec((1, n), lambda i: (0, 0)))
        args.append(g_final.reshape(1, n).astype(F32))
        body = _ffn_down_final_body
    return pl.pallas_call(
        body,
        grid=(m // tm,),
        in_specs=in_specs,
        out_specs=row(n),
        out_shape=jax.ShapeDtypeStruct((m, n), F32),
        input_output_aliases={2: 0},
        compiler_params=_cparams("arbitrary"),
        name="ffn_down",
    )(*args)


def _swiglu_body(x_ref, g_ref, wg_ref, wu_ref, o_ref, h_ref):
    @pl.when(pl.program_id(1) == 0)
    def _():
        h_ref[...] = _normed_bf16(x_ref, g_ref)

    h = h_ref[...]
    g = jnp.dot(h, wg_ref[...], preferred_element_type=F32)
    u = jnp.dot(h, wu_ref[...], preferred_element_type=F32)
    o_ref[...] = ((g * jax.nn.sigmoid(g)) * u).astype(o_ref.dtype)


def _swiglu_up(x, g, wg, wu):
    m, k = x.shape
    n = wg.shape[1]
    tm, tn = _tile(m, 1024), _tile(n, 512)
    return pl.pallas_call(
        _swiglu_body,
        grid=(m // tm, n // tn),
        in_specs=[pl.BlockSpec((tm, k), lambda i, j: (i, 0)),
                  pl.BlockSpec((1, k), lambda i, j: (0, 0)),
                  pl.BlockSpec((k, tn), lambda i, j: (0, j)),
                  pl.BlockSpec((k, tn), lambda i, j: (0, j))],
        out_specs=pl.BlockSpec((tm, tn), lambda i, j: (i, j)),
        out_shape=jax.ShapeDtypeStruct((m, n), BF16),
        scratch_shapes=[pltpu.VMEM((tm, k), BF16)],
        compiler_params=_cparams("parallel", "arbitrary"),
        name="swiglu_up",
    )(x, g.reshape(1, k).astype(F32), wg, wu)


def _merge_out_body(hm_ref, ha_ref, gm0_ref, gm1_ref, ga0_ref, ga1_ref, pm_ref, pa_ref, wo_ref, x_ref, o_ref):
    ym = jnp.dot(hm_ref[...], pm_ref[...], preferred_element_type=F32)
    ya = jnp.dot(ha_ref[...], pa_ref[...], preferred_element_type=F32)
    gm = jax.nn.sigmoid(jnp.concatenate([gm0_ref[...], gm1_ref[...]], axis=1).astype(F32))
    ga = jax.nn.sigmoid(jnp.concatenate([ga0_ref[...], ga1_ref[...]], axis=1).astype(F32))
    y = (gm * ym + ga * ya).astype(BF16)
    o_ref[...] = x_ref[...] + jnp.dot(y, wo_ref[...], preferred_element_type=F32)


def _merge_out(h_m, h_a, proj, p_m, p_a, w_out, x, in_place):
    m, d = x.shape
    tm = _tile(m, 512)
    gw = d // 2
    assert C_GM % gw == 0 and p_m.shape[1] == d and p_a.shape[1] == d
    g0 = C_GM // gw
    row = lambda w: pl.BlockSpec((tm, w), lambda i: (i, 0))
    gate = lambda jj: pl.BlockSpec((tm, gw), lambda i: (i, g0 + jj))
    resident = lambda a: pl.BlockSpec(a.shape, lambda i: (0, 0), pipeline_mode=pl.Buffered(1))
    return pl.pallas_call(
        _merge_out_body,
        grid=(m // tm,),
        in_specs=[row(h_m.shape[1]), row(h_a.shape[1]), gate(0), gate(1), gate(2), gate(3),
                  resident(p_m), resident(p_a), resident(w_out), row(d)],
        out_specs=row(d),
        out_shape=jax.ShapeDtypeStruct((m, d), F32),
        input_output_aliases={9: 0} if in_place else {},
        compiler_params=_cparams("arbitrary"),
        name="merge_out",
    )(h_m, h_a, proj, proj, proj, proj, p_m, p_a, w_out, x)


_ROPE_HALF = ROPE_DIM // 2
ROPE_LANE_TO_DIM = (list(range(0, _ROPE_HALF)) + list(range(ROPE_DIM, LANES // 2 + _ROPE_HALF))
                    + list(range(_ROPE_HALF, ROPE_DIM)) + list(range(LANES // 2 + _ROPE_HALF, A_DH)))


def _rope_table_body(pos_ref, inv_ref, cos_ref, sin_ref):
    ang = pos_ref[...].astype(F32) * inv_ref[...]
    lane = lax.broadcasted_iota(jnp.int32, ang.shape, 1)
    s = jnp.sin(ang)
    cos_ref[...] = jnp.cos(ang)
    sin_ref[...] = jnp.where(lane < LANES // 2, -s, s)


def _rope_tables(positions):
    m = positions.size
    inv = jnp.power(ROPE_THETA, -jnp.arange(_ROPE_HALF, dtype=F32) * (2.0 / ROPE_DIM))
    inv_lane = (jnp.zeros((1, LANES), F32).at[0, :_ROPE_HALF].set(inv)
                .at[0, LANES // 2:LANES // 2 + _ROPE_HALF].set(inv))
    ts = _tile(m, 2048)
    spec = pl.BlockSpec((ts, LANES), lambda i: (i, 0))
    shape = jax.ShapeDtypeStruct((m, LANES), F32)
    return pl.pallas_call(
        _rope_table_body,
        grid=(m // ts,),
        in_specs=[pl.BlockSpec((ts, 1), lambda i: (i, 0)), pl.BlockSpec((1, LANES), lambda i: (0, 0))],
        out_specs=[spec, spec],
        out_shape=[shape, shape],
        compiler_params=_cparams("parallel"),
        name="rope_tables",
    )(positions.reshape(m, 1), inv_lane)


def _conv_silu_body(x_ref, w_ref, b_ref, o_ref, *, k_blocks_from, k_scale):
    x = x_ref[0].astype(F32)
    w = w_ref[...]
    row = lax.broadcasted_iota(jnp.int32, x.shape, 0)
    y = x * w[M_CONV - 1:M_CONV, :] + b_ref[...]
    for d in range(1, M_CONV):
        xd = jnp.where(row >= d, pltpu.roll(x, d, 0), 0.0)
        y = y + xd * w[M_CONV - 1 - d:M_CONV - d, :]
    y = y * jax.nn.sigmoid(y)
    mul = jnp.where(pl.program_id(1) >= k_blocks_from, k_scale, 1.0).astype(F32)
    o_ref[0] = (y * mul).astype(o_ref.dtype)


def _conv_silu(proj3, conv_w, conv_b):
    b, s, _ = proj3.shape
    nblk = M_QK_W // LANES
    body = functools.partial(_conv_silu_body, k_blocks_from=nblk // 2, k_scale=M_DQK ** -0.5)
    return pl.pallas_call(
        body,
        grid=(b, nblk),
        in_specs=[pl.BlockSpec((1, s, LANES), lambda i, j: (i, 0, C_MQK // LANES + j)),
                  pl.BlockSpec((M_CONV, LANES), lambda i, j: (0, j)),
                  pl.BlockSpec((1, LANES), lambda i, j: (0, j))],
        out_specs=pl.BlockSpec((1, s, LANES), lambda i, j: (i, 0, j)),
        out_shape=jax.ShapeDtypeStruct((b, s, M_QK_W), BF16),
        compiler_params=_cparams("parallel", "arbitrary"),
        name="conv_silu",
    )(proj3, conv_w.astype(F32), conv_b.reshape(1, M_QK_W).astype(F32))


def _log_sigmoid(x):
    return jnp.minimum(x, 0.0) - jnp.log1p(jnp.exp(-jnp.abs(x)))


def _mlstm_body(qk_ref, v_ref, o_ref, gc_ref, gr_ref, bc_ref, br_ref, gn_ref, out_ref,
                c_ref, n_ref, m_ref, *, chunk):
    L = chunk

    @pl.when(pl.program_id(1) == 0)
    def _():
        c_ref[...] = jnp.zeros_like(c_ref)
        n_ref[...] = jnp.zeros_like(n_ref)
        m_ref[...] = jnp.zeros_like(m_ref)

    gc = gc_ref[0] + bc_ref[...]
    gr = gr_ref[0] + br_ref[...]
    lf_all_c = _log_sigmoid(gc)
    lf_all_r = _log_sigmoid(gr)
    row = lax.broadcasted_iota(jnp.int32, (L, L), 0)
    col = lax.broadcasted_iota(jnp.int32, (L, L), 1)
    causal = col <= row
    for h in range(M_HEADS):
        li_c = gc[:, h:h + 1]
        lf_c = lf_all_c[:, M_HEADS + h:M_HEADS + h + 1]
        li_r = gr[h:h + 1, :]
        lf_r = lf_all_r[M_HEADS + h:M_HEADS + h + 1, :]
        b_c = jnp.sum(jnp.where(causal, lf_r, 0.0), axis=1, keepdims=True)
        b_r = jnp.sum(jnp.where(row <= col, lf_c, 0.0), axis=0, keepdims=True)
        g = jnp.sum(lf_r, axis=1, keepdims=True)
        m_prev = m_ref[h][0:1, 0:1]
        dmat = jnp.where(causal, b_c - b_r + li_r, -jnp.inf)
        m_inter = b_c + m_prev
        m_t = jnp.maximum(jnp.max(dmat, axis=1, keepdims=True), m_inter)
        w_intra = jnp.exp(dmat - m_t)
        w_inter = jnp.exp(m_inter - m_t)
        q = qk_ref[0, :, h * M_DQK:(h + 1) * M_DQK]
        k = qk_ref[0, :, M_QK_W // 2 + h * M_DQK:M_QK_W // 2 + (h + 1) * M_DQK]
        v = v_ref[0, :, h * M_DV:(h + 1) * M_DV]
        s = lax.dot_general(q, k, (((1,), (1,)), ((), ())), preferred_element_type=F32) * w_intra
        c_old = c_ref[h]
        n_old = n_ref[h][0:1, :]
        num = (jnp.dot(s.astype(BF16), v, preferred_element_type=F32)
               + w_inter * jnp.dot(q, c_old.astype(BF16), preferred_element_type=F32))
        den = (jnp.sum(s, axis=1, keepdims=True)
               + w_inter * jnp.sum(q.astype(F32) * n_old, axis=1, keepdims=True))
        ht = num / jnp.maximum(jnp.abs(den), jnp.exp(-m_t))
        a_c = g - b_c + li_c
        m_new = jnp.maximum(g + m_prev, jnp.max(a_c, axis=0, keepdims=True))
        decay = jnp.exp(g + m_prev - m_new)
        kw = k.astype(F32) * jnp.exp(a_c - m_new)
        c_ref[h] = decay * c_old + lax.dot_general(
            kw.astype(BF16), v, (((0,), (0,)), ((), ())), preferred_element_type=F32)
        n_ref[h] = jnp.broadcast_to(decay * n_old + jnp.sum(kw, axis=0, keepdims=True), n_ref.shape[1:])
        m_ref[h] = jnp.broadcast_to(m_new, m_ref.shape[1:])
        hn = ht * lax.rsqrt(jnp.mean(ht * ht, axis=-1, keepdims=True) + EPS)
        hn = hn * gn_ref[:, h * M_DV:(h + 1) * M_DV]
        og = jax.nn.sigmoid(o_ref[0, :, h * M_DV:(h + 1) * M_DV].astype(F32))
        out_ref[0, :, h * M_DV:(h + 1) * M_DV] = (og * hn).astype(out_ref.dtype)


def _mlstm(qk, proj3, gates_c, gates_r, bias_c, bias_r, g_mhead, chunk):
    b, s, _ = proj3.shape
    L = chunk
    body = functools.partial(_mlstm_body, chunk=L)
    return pl.pallas_call(
        body,
        grid=(b, s // L),
        in_specs=[pl.BlockSpec((1, L, M_QK_W), lambda i, c: (i, c, 0)),
                  pl.BlockSpec((1, L, M_V_W), lambda i, c: (i, c, C_MV // M_V_W)),
                  pl.BlockSpec((1, L, M_V_W), lambda i, c: (i, c, C_MO // M_V_W)),
                  pl.BlockSpec((1, L, LANES), lambda i, c: (i, c, 0)),
                  pl.BlockSpec((1, 2 * M_HEADS, L), lambda i, c: (i, 0, c)),
                  pl.BlockSpec((1, LANES), lambda i, c: (0, 0)),
                  pl.BlockSpec((2 * M_HEADS, 1), lambda i, c: (0, 0)),
                  pl.BlockSpec((1, M_V_W), lambda i, c: (0, 0))],
        out_specs=pl.BlockSpec((1, L, M_V_W), lambda i, c: (i, c, 0)),
        out_shape=jax.ShapeDtypeStruct((b, s, M_V_W), BF16),
        scratch_shapes=[pltpu.VMEM((M_HEADS, M_DQK, M_DV), F32),
                        pltpu.VMEM((M_HEADS, 8, M_DQK), F32),
                        pltpu.VMEM((M_HEADS, 8, LANES), F32)],
        compiler_params=_cparams("parallel", "arbitrary"),
        name="mlstm",
    )(qk, proj3, proj3, gates_c, gates_r, bias_c, bias_r, g_mhead.reshape(1, M_V_W).astype(F32))


def _attn_body(q_ref, k_ref, v_ref, lam_ref, g_ref, o_ref, acc_ref, st_ref, pt_ref, *, lam_init):
    qi = pl.program_id(2)
    q = q_ref[0]
    tq = q.shape[0]
    tk = tq
    acc_ref[...] = jnp.zeros_like(acc_ref)
    init = ((jnp.full((1, tq), -jnp.inf, F32), jnp.zeros((1, tq), F32)),) * 2

    def scores(j, slot):
        k_blk = k_ref[0, pl.ds(pl.multiple_of(j * tk, tk), tk), :]
        for c in range(2):
            sl = slice(c * A_DH, (c + 1) * A_DH)
            st_ref[slot, c] = lax.dot_general(k_blk[:, sl], q[:, sl], (((1,), (1,)), ((), ())),
                                              preferred_element_type=F32)

    def softmax(slot, masked, stats):
        if masked:
            keep = (lax.broadcasted_iota(jnp.int32, (tk, tq), 0)
                    <= lax.broadcasted_iota(jnp.int32, (tk, tq), 1))
        new_stats, alphas = [], []
        for c in range(2):
            st = st_ref[slot, c]
            if masked:
                st = jnp.where(keep, st, -jnp.inf)
            m_old, l_old = stats[c]
            m_new = jnp.maximum(m_old, jnp.max(st, axis=0, keepdims=True))
            alpha = jnp.exp2(m_old - m_new)
            pt = jnp.exp2(st - m_new)
            pt_ref[slot, c] = pt.astype(BF16)
            new_stats.append((m_new, alpha * l_old + jnp.sum(pt, axis=0, keepdims=True)))
            alphas.append(alpha)
        return tuple(new_stats), tuple(alphas)

    def values(j, slot, alphas):
        v_blk = v_ref[0, pl.ds(pl.multiple_of(j * tk, tk), tk), :]
        for c in range(2):
            acc_ref[c] = alphas[c] * acc_ref[c] + lax.dot_general(
                v_blk, pt_ref[slot, c], (((0,), (0,)), ((), ())), preferred_element_type=F32)

    def tick(t, slot, carry):
        stats, alphas = carry
        scores(t, slot)
        carry = softmax(1 - slot, False, stats)
        values(t - 2, slot, alphas)
        return carry

    def one_block(stats):
        scores(0, 0)
        stats, alphas = softmax(0, True, stats)
        values(0, 0, alphas)
        return stats

    def drain(slot, carry):
        stats, alphas = carry
        values(qi - 1, 1 - slot, alphas)
        stats, alphas = softmax(slot, True, stats)
        values(qi, slot, alphas)
        return stats

    def several_blocks(stats):
        scores(0, 0)
        scores(1, 1)
        carry = softmax(0, False, stats)
        n_pairs = lax.shift_right_logical(qi - 1, 1)
        carry = lax.fori_loop(0, n_pairs, lambda u, cr: tick(3 + 2 * u, 1, tick(2 + 2 * u, 0, cr)), carry)
        qi_even = (qi & 1) == 0

        def unpaired_tick(carry):
            stats, alphas = carry
            values(qi - 2, 0, alphas)
            scores(qi, 0)
            return softmax(1, False, stats)

        carry = lax.cond(qi_even, unpaired_tick, lambda cr: cr, carry)
        return lax.cond(qi_even, functools.partial(drain, 0), functools.partial(drain, 1), carry)

    (_, l0), (_, l1) = lax.cond(qi >= 1, several_blocks, one_block, init)

    lv = lam_ref[...]
    lam = (jnp.exp(jnp.sum(lv[0:1] * lv[1:2], axis=1, keepdims=True))
           - jnp.exp(jnp.sum(lv[2:3] * lv[3:4], axis=1, keepdims=True)) + lam_init)
    o = acc_ref[0] / l0 - lam * (acc_ref[1] / l1)
    y = (o * lax.rsqrt(jnp.mean(o * o, axis=0, keepdims=True) + EPS)) * g_ref[...]
    o_ref[0] = (y * (1.0 - lam_init)).T.astype(o_ref.dtype)


def _diff_attention(proj3, lam_vecs, g_sub, lam_init, tq):
    b, s, _ = proj3.shape
    w = 2 * A_DH
    q0 = C_AQ // w
    k0 = C_AK // w
    v0 = C_AV // w
    return pl.pallas_call(
        functools.partial(_attn_body, lam_init=lam_init),
        grid=(b, A_HEADS, s // tq),
        in_specs=[pl.BlockSpec((1, tq, w), lambda i, h, qi: (i, qi, q0 + h)),
                  pl.BlockSpec((1, s, w), lambda i, h, qi: (i, 0, k0 + h)),
                  pl.BlockSpec((1, s, A_DV), lambda i, h, qi: (i, 0, v0 + h)),
                  pl.BlockSpec((4, A_DH), lambda i, h, qi: (0, 0)),
                  pl.BlockSpec((A_DV, 1), lambda i, h, qi: (0, 0))],
        out_specs=pl.BlockSpec((1, tq, A_DV), lambda i, h, qi: (i, qi, h)),
        out_shape=jax.ShapeDtypeStruct((b, s, A_V_W), BF16),
        scratch_shapes=[pltpu.VMEM((2, A_DV, tq), F32),
                        pltpu.VMEM((2, 2, tq, tq), F32),
                        pltpu.VMEM((2, 2, tq, tq), BF16)],
        compiler_params=_cparams("parallel", "parallel", "arbitrary"),
        name="diff_attention",
    )(proj3, proj3, proj3, lam_vecs, g_sub.reshape(A_DV, 1).astype(F32))


def kernel(x, positions, g_mix, w_in, conv_w, conv_b, i_bias, f_bias, g_mhead, lambda_q1, lambda_k1,
           lambda_q2, lambda_k2, g_sub, p_m, p_a, w_out, g_ffn, w_gate, w_up, w_down, g_final):
    B, S, D = x.shape
    M = B * S
    xf = x.reshape(M, D)
    tables = _rope_tables(positions)
    gate0 = C_AQ
    lane_to_dim = jnp.array(ROPE_LANE_TO_DIM, jnp.int32)
    chunk = _tile(S, 256)
    tq = _tile(S, 512)
    for l in range(DEPTH):
        lam_init = 0.8 - 0.6 * math.exp(-0.3 * l)
        w = w_in[l]
        qk0 = gate0 + 2 * M_HEADS
        w_qk = w[:, qk0:qk0 + 2 * A_Q_W].reshape(D, 2 * A_Q_W // A_DH, A_DH)[:, :, lane_to_dim]
        w_main = jnp.concatenate([w[:, :gate0], w_qk.reshape(D, 2 * A_Q_W), w[:, qk0 + 2 * A_Q_W:]],
                                 axis=1).astype(BF16)
        w_gates = jnp.pad(w[:, gate0:gate0 + 2 * M_HEADS], ((0, 0), (0, LANES - 2 * M_HEADS))).astype(BF16)

        proj, gates = _in_proj(xf, g_mix[l], w_main, w_gates, tables)
        proj3 = proj.reshape(B, S, PROJ_W)

        qk = _conv_silu(proj3, conv_w[l], conv_b[l])
        gates_c = gates.reshape(B, S, LANES)
        gates_r = jnp.swapaxes(gates_c[:, :, :2 * M_HEADS], 1, 2)
        bias8 = jnp.concatenate([i_bias[l], f_bias[l]]).astype(F32)
        bias_c = jnp.pad(bias8, (0, LANES - 2 * M_HEADS)).reshape(1, LANES)
        h_m = _mlstm(qk, proj3, gates_c, gates_r, bias_c, bias8.reshape(2 * M_HEADS, 1), g_mhead[l], chunk)

        lam_vecs = jnp.stack([lambda_q1[l], lambda_k1[l], lambda_q2[l], lambda_k2[l]]).astype(F32)
        h_a = _diff_attention(proj3, lam_vecs, g_sub[l], lam_init, tq)

        xf = _merge_out(h_m.reshape(M, M_V_W), h_a.reshape(M, A_V_W), proj,
                        p_m[l].astype(BF16), p_a[l].astype(BF16), w_out[l].astype(BF16), xf, in_place=l > 0)

        hh = _swiglu_up(xf, g_ffn[l], w_gate[l].astype(BF16), w_up[l].astype(BF16))
        xf = _ffn_down(hh, w_down[l].astype(BF16), xf, g_final if l == DEPTH - 1 else None)
    return xf.astype(x.dtype).reshape(B, S, D)
```
